```python
import math
import jax, jax.numpy as jnp
from jax import lax
import numpy as np

D_MODEL = 4096
BATCH = 1
SEQ = 8192
DEPTH = 2

N_MEM = 256
EPS = 1e-5
BLOCK = 128

CONV_CH = 2048
CONV_WIDTH = 31
CONV_GROUPS = 16

DIFF_HEADS = 8
DIFF_HD = 128
DIFF_QK = DIFF_HEADS * 2 * DIFF_HD
DIFF_V = DIFF_HEADS * 2 * DIFF_HD

SGU_CH = 2048
SGU_GROUPS = 8
SGU_CHUNK = 128

SWA_HEADS = 32
SWA_KV = 4
SWA_HD = 64
WINDOW = 128

XA_HEADS = 4
XA_HD = 256

N_GROUPS = 8
EXP_PER_GROUP = 8
N_EXPERTS = N_GROUPS * EXP_PER_GROUP
TOP_K = 2
D_FF = 384
MOE_BLOCK = 128

EVEN_IN = 2 * CONV_CH + 2 * DIFF_QK + DIFF_V
EVEN_OUT = CONV_CH + DIFF_V
ODD_IN = 2 * SGU_CH + SWA_HEADS * SWA_HD + 2 * SWA_KV * SWA_HD
ODD_OUT = SGU_CH + SWA_HEADS * SWA_HD
N_EVEN = (DEPTH + 1) // 2
N_ODD = DEPTH // 2

kernel_name = 'hybrid_conv_diffattn_sgu_swa_hmoe'


def rms_norm(x, g):
    xf = x.astype(jnp.float32)
    y = xf * lax.rsqrt(jnp.mean(xf * xf, axis=-1, keepdims=True) + EPS)
    return (y * g.astype(jnp.float32)).astype(x.dtype)


def group_layer_norm(x, n_groups, g, b):
    shp = x.shape
    xf = x.astype(jnp.float32).reshape(shp[:-1] + (n_groups, shp[-1] // n_groups))
    mu = jnp.mean(xf, axis=-1, keepdims=True)
    var = jnp.mean(jnp.square(xf - mu), axis=-1, keepdims=True)
    y = ((xf - mu) * lax.rsqrt(var + EPS)).reshape(shp)
    return (y * g.astype(jnp.float32) + b.astype(jnp.float32)).astype(x.dtype)


def conformer_conv(a_in, conv_w, conv_b, norm_g, norm_b):
    a, gate = jnp.split(a_in, 2, axis=-1)
    y = a * jax.nn.sigmoid(gate)
    y = lax.conv_general_dilated(
        y, conv_w[:, None, :], window_strides=(1,), padding=[(CONV_WIDTH - 1, 0)],
        dimension_numbers=('NWC', 'WIO', 'NWC'), feature_group_count=CONV_CH) + conv_b
    y = group_layer_norm(y, CONV_GROUPS, norm_g, norm_b)
    return jax.nn.silu(y)


def diff_attention(q, k, v, lam_vecs, subln_g, lambda_init):
    B, S, _ = q.shape
    H, d = DIFF_HEADS, DIFF_HD
    nb = S // BLOCK
    q = q.reshape(B, S, H, 2, d)
    k = k.reshape(B, S, H, 2, d)
    v = v.reshape(B, S, H, 2 * d)
    lf = lam_vecs.astype(jnp.float32)
    lam = jnp.exp(jnp.sum(lf[0] * lf[1])) - jnp.exp(jnp.sum(lf[2] * lf[3])) + lambda_init
    scale = d ** -0.5
    key_pos = jnp.arange(S)
    q_blocks = jnp.moveaxis(q.reshape(B, nb, BLOCK, H, 2, d), 1, 0)

    def one_block(args):
        qi, bi = args
        s = jnp.einsum('bqhcd,bkhcd->bhcqk', qi, k,
                       preferred_element_type=jnp.float32) * scale
        q_pos = bi * BLOCK + jnp.arange(BLOCK)
        causal = key_pos[None, :] <= q_pos[:, None]
        p = jax.nn.softmax(jnp.where(causal, s, -jnp.inf), axis=-1)
        attn = p[:, :, 0] - lam * p[:, :, 1]
        return jnp.einsum('bhqk,bkhe->bqhe', attn.astype(v.dtype), v)

    o = lax.map(one_block, (q_blocks, jnp.arange(nb)))
    o = jnp.moveaxis(o, 0, 1).reshape(B, S, H, 2 * d)
    o = rms_norm(o, subln_g) * (1.0 - lambda_init)
    return o.reshape(B, S, H * 2 * d)


def spatial_gating(c_in, norm_g, norm_b, w_s, b_s):
    B, S, _ = c_in.shape
    nc = S // SGU_CHUNK
    z = jax.nn.gelu(c_in)
    u, v = jnp.split(z, 2, axis=-1)
    v = group_layer_norm(v, SGU_GROUPS, norm_g, norm_b)
    vc = v.reshape(B, nc, SGU_CHUNK, SGU_GROUPS, SGU_CH // SGU_GROUPS)
    causal = jnp.tril(jnp.ones((SGU_CHUNK, SGU_CHUNK), dtype=bool))
    w = jnp.where(causal, w_s, 0.0).astype(vc.dtype)
    mixed = jnp.einsum('gts,bcsgd->bctgd', w, vc) + b_s.T[:, :, None].astype(vc.dtype)
    uc = u.reshape(B, nc, SGU_CHUNK, SGU_GROUPS, SGU_CH // SGU_GROUPS)
    return (uc * mixed).reshape(B, S, SGU_CH)


def sliding_window_attention(q, k, v, sinks):
    B, S, _ = q.shape
    nb = S // BLOCK
    G = SWA_HEADS // SWA_KV
    d = SWA_HD
    qb = q.reshape(B, nb, BLOCK, SWA_KV, G, d)
    kb = k.reshape(B, nb, BLOCK, SWA_KV, d)
    vb = v.reshape(B, nb, BLOCK, SWA_KV, d)

    def with_prev(t):
        prev = jnp.concatenate([jnp.zeros_like(t[:, :1]), t[:, :-1]], axis=1)
        return jnp.concatenate([prev, t], axis=2)

    kk, vv = with_prev(kb), with_prev(vb)
    s = jnp.einsum('bnqhgd,bnkhd->bnhgqk', qb, kk,
                   preferred_element_type=jnp.float32) * d ** -0.5
    qi = jnp.arange(BLOCK)[:, None]
    kj = jnp.arange(2 * BLOCK)[None, :]
    rel = qi + BLOCK - kj
    blk = jnp.arange(nb)[:, None, None]
    valid = (rel >= 0) & (rel < WINDOW) & (blk * BLOCK + kj - BLOCK >= 0)
    s = jnp.where(valid[None, :, None, None], s, -jnp.inf)
    sink = jnp.broadcast_to(sinks.astype(jnp.float32).reshape(SWA_KV, G, 1, 1), s.shape[:-1] + (1,))
    p = jax.nn.softmax(jnp.concatenate([s, sink], axis=-1), axis=-1)[..., :-1]
    o = jnp.einsum('bnhgqk,bnkhd->bnqhgd', p.astype(v.dtype), vv)
    return o.reshape(B, S, SWA_HEADS * d)


def memory_cross_attention(hn, mem_n, wq, wkv, wo):
    B, S, _ = hn.shape
    M = mem_n.shape[1]
    q = (hn @ wq).reshape(B, S, XA_HEADS, XA_HD)
    kv = (mem_n @ wkv).reshape(B, M, 2, XA_HEADS, XA_HD)
    k, v = kv[:, :, 0], kv[:, :, 1]
    s = jnp.einsum('bqhd,bmhd->bhqm', q, k, preferred_element_type=jnp.float32) * XA_HD ** -0.5
    p = jax.nn.softmax(s, axis=-1)
    o = jnp.einsum('bhqm,bmhd->bqhd', p.astype(v.dtype), v).reshape(B, S, XA_HEADS * XA_HD)
    return o @ wo


def hierarchical_moe(hn, wr_g, br_g, wr_e, br_e, w_gate, w_up, w_down):
    B, S, D = hn.shape
    xt = hn.reshape(-1, D)
    N = xt.shape[0]
    g_logits = jnp.einsum('nd,dg->ng', xt, wr_g, preferred_element_type=jnp.float32) + br_g
    g_prob = jax.nn.softmax(g_logits, axis=-1)
    g_sel = jnp.argmax(g_logits, axis=-1)
    g_w = jnp.take_along_axis(g_prob, g_sel[:, None], axis=-1)
    e_logits_all = jnp.einsum('nd,gde->nge', xt, wr_e,
                              preferred_element_type=jnp.float32) + br_e
    e_logits = jnp.take_along_axis(e_logits_all, g_sel[:, None, None], axis=1)[:, 0]
    e_prob = jax.nn.softmax(e_logits, axis=-1)
    top_p, top_i = lax.top_k(e_prob, TOP_K)
    top_w = g_w * top_p / jnp.sum(top_p, axis=-1, keepdims=True)
    expert_id = g_sel[:, None] * EXP_PER_GROUP + top_i

    A = N * TOP_K
    flat_e = expert_id.reshape(-1)
    flat_tok = jnp.repeat(jnp.arange(N), TOP_K)
    flat_w = top_w.reshape(-1)
    order = jnp.argsort(flat_e)
    se, stok, sw = flat_e[order], flat_tok[order], flat_w[order]
    counts = jnp.zeros((N_EXPERTS,), jnp.int32).at[flat_e].add(1)
    start = jnp.cumsum(counts) - counts
    padded = ((counts + MOE_BLOCK - 1) // MOE_BLOCK) * MOE_BLOCK
    pend = jnp.cumsum(padded)
    pstart = pend - padded
    pos = pstart[se] + (jnp.arange(A) - start[se])
    nblk = -(-A // MOE_BLOCK) + N_EXPERTS
    buf = jnp.zeros((nblk * MOE_BLOCK, D), hn.dtype).at[pos].set(xt[stok])
    blk_e = jnp.minimum(jnp.searchsorted(pend, jnp.arange(nblk) * MOE_BLOCK, side='right'),
                        N_EXPERTS - 1)

    def expert_block(args):
        xb, e = args
        hb = jax.nn.silu(xb @ w_gate[e]) * (xb @ w_up[e])
        return hb @ w_down[e]

    yb = lax.map(expert_block, (buf.reshape(nblk, MOE_BLOCK, D), blk_e))
    ys = yb.reshape(-1, D)[pos] * sw[:, None].astype(hn.dtype)
    y = jax.ops.segment_sum(ys, stok, num_segments=N)
    return y.reshape(B, S, D)


def setup_inputs(seed: int = 0) -> dict:
    key = jax.random.key(seed)
    keys = list(jax.random.split(key, 48))

    def nrm(shape, scale):
        return jax.random.normal(keys.pop(), shape, jnp.float32) * scale

    def gain(shape):
        return 1.0 + nrm(shape, 0.02)

    D = D_MODEL
    return {
        'x': nrm((BATCH, SEQ, D), 1.0),
        'mem': nrm((BATCH, N_MEM, D), 1.0),
        'mix_norm': gain((DEPTH, D)),
        'even_w_in': nrm((N_EVEN, D, EVEN_IN), D ** -0.5),
        'conv_w': nrm((N_EVEN, CONV_WIDTH, CONV_CH), CONV_WIDTH ** -0.5),
        'conv_b': nrm((N_EVEN, CONV_CH), 0.01),
        'conv_norm_g': gain((N_EVEN, CONV_CH)),
        'conv_norm_b': nrm((N_EVEN, CONV_CH), 0.01),
        'diff_lambda': nrm((N_EVEN, 4, DIFF_HD), 0.1),
        'diff_subln_g': gain((N_EVEN, 2 * DIFF_HD)),
        'even_w_out': nrm((N_EVEN, EVEN_OUT, D), EVEN_OUT ** -0.5),
        'odd_w_in': nrm((N_ODD, D, ODD_IN), D ** -0.5),
        'sgu_norm_g': gain((N_ODD, SGU_CH)),
        'sgu_norm_b': nrm((N_ODD, SGU_CH), 0.01),
        'sgu_w': nrm((N_ODD, SGU_GROUPS, SGU_CHUNK, SGU_CHUNK), SGU_CHUNK ** -0.5),
        'sgu_b': gain((N_ODD, SGU_GROUPS, SGU_CHUNK)),
        'swa_sinks': nrm((N_ODD, SWA_HEADS), 0.5),
        'odd_w_out': nrm((N_ODD, ODD_OUT, D), ODD_OUT ** -0.5),
        'xa_norm': gain((DEPTH, D)),
        'mem_norm': gain((DEPTH, D)),
        'xa_wq': nrm((DEPTH, D, XA_HEADS * XA_HD), D ** -0.5),
        'xa_wkv': nrm((DEPTH, D, 2 * XA_HEADS * XA_HD), D ** -0.5),
        'xa_wo': nrm((DEPTH, XA_HEADS * XA_HD, D), (XA_HEADS * XA_HD) ** -0.5),
        'moe_norm': gain((DEPTH, D)),
        'router_group_w': nrm((DEPTH, D, N_GROUPS), D ** -0.5),
        'router_group_b': nrm((DEPTH, N_GROUPS), 0.01),
        'router_expert_w': nrm((DEPTH, N_GROUPS, D, EXP_PER_GROUP), D ** -0.5),
        'router_expert_b': nrm((DEPTH, N_GROUPS, EXP_PER_GROUP), 0.01),
        'w_gate': nrm((DEPTH, N_EXPERTS, D, D_FF), D ** -0.5),
        'w_up': nrm((DEPTH, N_EXPERTS, D, D_FF), D ** -0.5),
        'w_down': nrm((DEPTH, N_EXPERTS, D_FF, D), D_FF ** -0.5),
        'final_norm': gain((D,)),
    }


def reference(x, mem, mix_norm, even_w_in, conv_w, conv_b, conv_norm_g, conv_norm_b,
              diff_lambda, diff_subln_g, even_w_out, odd_w_in, sgu_norm_g, sgu_norm_b,
              sgu_w, sgu_b, swa_sinks, odd_w_out, xa_norm, mem_norm, xa_wq, xa_wkv, xa_wo,
              moe_norm, router_group_w, router_group_b, router_expert_w, router_expert_b,
              w_gate, w_up, w_down, final_norm):
    h = x
    for layer in range(DEPTH):
        i = layer // 2
        hn = rms_norm(h, mix_norm[layer])
        if layer % 2 == 0:
            proj = hn @ even_w_in[i]
            a_in, q, k, v = jnp.split(
                proj, [2 * CONV_CH, 2 * CONV_CH + DIFF_QK, 2 * CONV_CH + 2 * DIFF_QK], axis=-1)
            y_a = conformer_conv(a_in, conv_w[i], conv_b[i], conv_norm_g[i], conv_norm_b[i])
            lambda_init = 0.8 - 0.6 * math.exp(-0.3 * layer)
            y_b = diff_attention(q, k, v, diff_lambda[i], diff_subln_g[i], lambda_init)
            mixed = jnp.concatenate([y_a, y_b], axis=-1) @ even_w_out[i]
        else:
            proj = hn @ odd_w_in[i]
            q_end = 2 * SGU_CH + SWA_HEADS * SWA_HD
            c_in, q, k, v = jnp.split(
                proj, [2 * SGU_CH, q_end, q_end + SWA_KV * SWA_HD], axis=-1)
            y_c = spatial_gating(c_in, sgu_norm_g[i], sgu_norm_b[i], sgu_w[i], sgu_b[i])
            y_d = sliding_window_attention(q, k, v, swa_sinks[i])
            mixed = jnp.concatenate([y_c, y_d], axis=-1) @ odd_w_out[i]
        h = h + mixed
        h = h + memory_cross_attention(rms_norm(h, xa_norm[layer]), rms_norm(mem, mem_norm[layer]),
                                       xa_wq[layer], xa_wkv[layer], xa_wo[layer])
        h = h + hierarchical_moe(rms_norm(h, moe_norm[layer]), router_group_w[layer],
                                 router_group_b[layer], router_expert_w[layer],
                                 router_expert_b[layer], w_gate[layer], w_up[layer], w_down[layer])
    return rms_norm(h, final_norm)
```

```python
import functools
import math

import jax
import jax.numpy as jnp
from jax import lax
from jax.experimental import pallas as pl
from jax.experimental.pallas import tpu as pltpu

F32 = jnp.float32
BF16 = jnp.bfloat16

D_MODEL = 4096
DEPTH = 2
EPS = 1e-5
BLOCK = 128

CONV_CH = 2048
CONV_WIDTH = 31
CONV_GROUPS = 16
CONV_HALO = 32

DIFF_HEADS = 8
DIFF_HD = 128
DIFF_QK = DIFF_HEADS * 2 * DIFF_HD
DIFF_V = DIFF_HEADS * 2 * DIFF_HD

SGU_CH = 2048
SGU_GROUPS = 8
SGU_CHUNK = 128

SWA_HEADS = 32
SWA_KV = 4
SWA_HD = 64
WINDOW = 128

XA_HEADS = 4
XA_HD = 256

N_GROUPS = 8
EXP_PER_GROUP = 8
N_EXPERTS = N_GROUPS * EXP_PER_GROUP
TOP_K = 2
D_FF = 384
MOE_BLOCK = 128

LANES = 128
LOG2E = math.log2(math.e)
VMEM_LIMIT = 56 * 1024 * 1024


def _params(sem, vmem=None):
    return pltpu.CompilerParams(dimension_semantics=sem, vmem_limit_bytes=vmem)


def _rmsnorm_kernel(x_ref, g_ref, o_ref):
    x = x_ref[...].astype(F32)
    ms = jnp.mean(x * x, axis=-1, keepdims=True)
    o_ref[...] = (x * lax.rsqrt(ms + EPS) * g_ref[...]).astype(o_ref.dtype)


def rmsnorm(x, g2, layer, out_dtype, tm=256):
    M, D = x.shape
    return pl.pallas_call(
        _rmsnorm_kernel,
        grid=(M // tm,),
        in_specs=[pl.BlockSpec((tm, D), lambda i: (i, 0)),
                  pl.BlockSpec((None, 1, D), lambda i: (layer, 0, 0))],
        out_specs=pl.BlockSpec((tm, D), lambda i: (i, 0)),
        out_shape=jax.ShapeDtypeStruct((M, D), out_dtype),
        compiler_params=_params(("arbitrary",)),
        name="rmsnorm",
    )(x, g2)


def _mm_kernel(*refs, k_splits, has_res):
    n_a = len(k_splits)
    a_refs = refs[:n_a]
    w_ref = refs[n_a]
    r_ref = refs[n_a + 1] if has_res else None
    o_ref = refs[n_a + 1 + has_res]
    wb_ref = refs[n_a + 2 + has_res]

    @pl.when(pl.program_id(1) == 0)
    def _():
        wb_ref[...] = w_ref[...].astype(BF16)

    acc = None
    off = 0
    for a_ref, ks in zip(a_refs, k_splits):
        part = jnp.dot(a_ref[...], wb_ref[off:off + ks, :], preferred_element_type=F32)
        acc = part if acc is None else acc + part
        off += ks
    if has_res:
        acc = acc + r_ref[...]
    o_ref[...] = acc.astype(o_ref.dtype)


def matmul(a_list, w3, layer, *, res=None, out_dtype, tm, tn):
    M = a_list[0].shape[0]
    _, K, N = w3.shape
    k_splits = tuple(a.shape[1] for a in a_list)
    assert sum(k_splits) == K and M % tm == 0 and N % tn == 0
    in_specs = [pl.BlockSpec((tm, ks), lambda j, i: (i, 0)) for ks in k_splits]
    in_specs.append(pl.BlockSpec((None, K, tn), lambda j, i: (layer, 0, j)))
    args = list(a_list) + [w3]
    if res is not None:
        in_specs.append(pl.BlockSpec((tm, tn), lambda j, i: (i, j)))
        args.append(res)
    return pl.pallas_call(
        functools.partial(_mm_kernel, k_splits=k_splits, has_res=res is not None),
        grid=(N // tn, M // tm),
        in_specs=in_specs,
        out_specs=pl.BlockSpec((tm, tn), lambda j, i: (i, j)),
        out_shape=jax.ShapeDtypeStruct((M, N), out_dtype),
        scratch_shapes=[pltpu.VMEM((K, tn), BF16)],
        compiler_params=_params(("arbitrary", "arbitrary"), VMEM_LIMIT),
        name="matmul",
    )(*args)


def _conv_kernel(a_ref, g_ref, w_ref, b_ref, ng_ref, nb_ref, o_ref, ybuf, *, T, LB):
    i = pl.program_id(1)

    @pl.when(i == 0)
    def _():
        ybuf[0:CONV_HALO, :] = jnp.zeros((CONV_HALO, LB), F32)

    @pl.when(i > 0)
    def _():
        ybuf[0:CONV_HALO, :] = ybuf[T:T + CONV_HALO, :]

    a = a_ref[...].astype(F32)
    g = g_ref[...].astype(F32)
    ybuf[CONV_HALO:CONV_HALO + T, :] = a * jax.nn.sigmoid(g)

    first = CONV_HALO - (CONV_WIDTH - 1)
    for gi in range(LB // LANES):
        cols = slice(gi * LANES, (gi + 1) * LANES)
        acc = jnp.broadcast_to(b_ref[:, cols], (T, LANES))
        for j in range(CONV_WIDTH):
            acc = acc + w_ref[j:j + 1, cols] * ybuf[first + j:first + j + T, cols]
        mu = jnp.mean(acc, axis=-1, keepdims=True)
        d = acc - mu
        var = jnp.mean(d * d, axis=-1, keepdims=True)
        y = d * lax.rsqrt(var + EPS) * ng_ref[:, cols] + nb_ref[:, cols]
        o_ref[:, cols] = (y * jax.nn.sigmoid(y)).astype(o_ref.dtype)


def conformer_conv(proj, conv_w, conv_b, norm_g, norm_b, li, T=256, LB=512):
    S = proj.shape[0]
    assert CONV_CH // CONV_GROUPS == LANES
    ncb = CONV_CH // LB
    vec = lambda: pl.BlockSpec((None, 1, LB), lambda c, i: (li, 0, c))
    return pl.pallas_call(
        functools.partial(_conv_kernel, T=T, LB=LB),
        grid=(ncb, S // T),
        in_specs=[pl.BlockSpec((T, LB), lambda c, i: (i, c)),
                  pl.BlockSpec((T, LB), lambda c, i: (i, ncb + c)),
                  pl.BlockSpec((None, CONV_WIDTH, LB), lambda c, i: (li, 0, c)),
                  vec(), vec(), vec()],
        out_specs=pl.BlockSpec((T, LB), lambda c, i: (i, c)),
        out_shape=jax.ShapeDtypeStruct((S, CONV_CH), BF16),
        scratch_shapes=[pltpu.VMEM((CONV_HALO + T, LB), F32)],
        compiler_params=_params(("arbitrary", "arbitrary")),
        name="conformer_conv",
    )(proj, proj, conv_w, conv_b, norm_g, norm_b)


def _diffattn_kernel(q_ref, k_ref, v_ref, lam_ref, g_ref, o_ref, acc_ref, *, tq, lambda_init):
    qi = pl.program_id(1)
    d = DIFF_HD
    c_exp = (d ** -0.5) * LOG2E
    q = q_ref[...]
    qs = (q[:, :d], q[:, d:])
    acc_ref[...] = jnp.zeros_like(acc_ref)

    def step(j, carry, masked):
        start = pl.multiple_of(j * tq, tq)
        kblk = k_ref[pl.ds(start, tq), :]
        vblk = v_ref[pl.ds(start, tq), :]
        new = []
        for c in range(2):
            m, l = carry[2 * c], carry[2 * c + 1]
            s = lax.dot_general(qs[c], kblk[:, c * d:(c + 1) * d], (((1,), (1,)), ((), ())),
                                preferred_element_type=F32)
            if masked:
                row = lax.broadcasted_iota(jnp.int32, (tq, tq), 0)
                col = lax.broadcasted_iota(jnp.int32, (tq, tq), 1)
                s = jnp.where(row >= col, s, -jnp.inf)
            m_new = jnp.maximum(m, jnp.max(s, axis=-1, keepdims=True))
            alpha = jnp.exp2((m - m_new) * c_exp)
            p = jnp.exp2((s - m_new) * c_exp)
            l_new = alpha * l + jnp.sum(p, axis=-1, keepdims=True)
            acc_ref[c] = alpha * acc_ref[c] + jnp.dot(p.astype(BF16), vblk,
                                                      preferred_element_type=F32)
            new += [m_new, l_new]
        return tuple(new)

    m_init = jnp.full((tq, 1), -jnp.inf, F32)
    l_init = jnp.zeros((tq, 1), F32)
    carry = lax.fori_loop(0, qi, lambda j, c: step(j, c, False), (m_init, l_init, m_init, l_init))
    _, l0, _, l1 = step(qi, carry, True)

    lv = lam_ref[...]
    lam = (jnp.exp(jnp.sum(lv[0:1] * lv[1:2], axis=-1, keepdims=True))
           - jnp.exp(jnp.sum(lv[2:3] * lv[3:4], axis=-1, keepdims=True)) + lambda_init)
    o = acc_ref[0] / l0 - lam * (acc_ref[1] / l1)
    ms = jnp.mean(o * o, axis=-1, keepdims=True)
    o = o * lax.rsqrt(ms + EPS) * g_ref[...] * (1.0 - lambda_init)
    o_ref[...] = o.astype(o_ref.dtype)


def diff_attention(proj, lam_vecs, subln_g3, li, lambda_init, tq=256):
    S = proj.shape[0]
    hw = 2 * DIFF_HD
    q0 = 2 * CONV_CH // hw
    k0 = q0 + DIFF_QK // hw
    v0 = k0 + DIFF_QK // hw
    return pl.pallas_call(
        functools.partial(_diffattn_kernel, tq=tq, lambda_init=lambda_init),
        grid=(DIFF_HEADS, S // tq),
        in_specs=[pl.BlockSpec((tq, hw), lambda h, i: (i, q0 + h)),
                  pl.BlockSpec((S, hw), lambda h, i: (0, k0 + h)),
                  pl.BlockSpec((S, hw), lambda h, i: (0, v0 + h)),
                  pl.BlockSpec((None, 4, DIFF_HD), lambda h, i: (li, 0, 0)),
                  pl.BlockSpec((None, 1, hw), lambda h, i: (li, 0, 0))],
        out_specs=pl.BlockSpec((tq, hw), lambda h, i: (i, h)),
        out_shape=jax.ShapeDtypeStruct((S, DIFF_V), BF16),
        scratch_shapes=[pltpu.VMEM((2, tq, hw), F32)],
        compiler_params=_params(("arbitrary", "arbitrary"), VMEM_LIMIT),
        name="diff_attention",
    )(proj, proj, proj, lam_vecs, subln_g3)


def _sgu_kernel(u_ref, v_ref, ng_ref, nb_ref, w_ref, bs_ref, o_ref):
    T = SGU_CHUNK
    gw = SGU_CH // SGU_GROUPS
    row = lax.broadcasted_iota(jnp.int32, (T, T), 0)
    col = lax.broadcasted_iota(jnp.int32, (T, T), 1)
    causal = row >= col
    for g in range(SGU_GROUPS):
        cols = slice(g * gw, (g + 1) * gw)
        u = jax.nn.gelu(u_ref[:, cols].astype(F32), approximate=True)
        v = jax.nn.gelu(v_ref[:, cols].astype(F32), approximate=True)
        mu = jnp.mean(v, axis=-1, keepdims=True)
        d = v - mu
        var = jnp.mean(d * d, axis=-1, keepdims=True)
        vn = d * lax.rsqrt(var + EPS) * ng_ref[:, cols] + nb_ref[:, cols]
        w = jnp.where(causal, w_ref[g], 0.0).astype(BF16)
        mixed = jnp.dot(w, vn.astype(BF16), preferred_element_type=F32) + bs_ref[:, g:g + 1]
        o_ref[:, cols] = (u * mixed).astype(o_ref.dtype)


def spatial_gating(proj, norm_g3, norm_b3, w_s, b_s_t, li):
    S = proj.shape[0]
    T = SGU_CHUNK
    vec = lambda: pl.BlockSpec((None, 1, SGU_CH), lambda i: (li, 0, 0))
    return pl.pallas_call(
        _sgu_kernel,
        grid=(S // T,),
        in_specs=[pl.BlockSpec((T, SGU_CH), lambda i: (i, 0)),
                  pl.BlockSpec((T, SGU_CH), lambda i: (i, 1)),
                  vec(), vec(),
                  pl.BlockSpec((None, SGU_GROUPS, T, T), lambda i: (li, 0, 0, 0)),
                  pl.BlockSpec((None, T, SGU_GROUPS), lambda i: (li, 0, 0))],
        out_specs=pl.BlockSpec((T, SGU_CH), lambda i: (i, 0)),
        out_shape=jax.ShapeDtypeStruct((S, SGU_CH), BF16),
        compiler_params=_params(("arbitrary",)),
        name="spatial_gating",
    )(proj, proj, norm_g3, norm_b3, w_s, b_s_t)


def _swa_kernel(sink_ref, q_ref, kp_ref, kc_ref, vp_ref, vc_ref, o_ref, *, li):
    n = pl.program_id(0)
    T = BLOCK
    G = SWA_HEADS // SWA_KV
    d = SWA_HD
    kk = jnp.concatenate([kp_ref[...], kc_ref[...]], axis=0)
    vv = jnp.concatenate([vp_ref[...], vc_ref[...]], axis=0)
    qi = lax.broadcasted_iota(jnp.int32, (T, 2 * T), 0)
    kj = lax.broadcasted_iota(jnp.int32, (T, 2 * T), 1)
    rel = qi + T - kj
    valid = (rel >= 0) & (rel < WINDOW) & ((n * T + kj - T) >= 0)
    scale = d ** -0.5
    for h in range(SWA_KV):
        kh = kk[:, h * d:(h + 1) * d]
        vh = vv[:, h * d:(h + 1) * d]
        outs = []
        for g in range(G):
            c0 = (h * G + g) * d
            s = lax.dot_general(q_ref[:, c0:c0 + d], kh, (((1,), (1,)), ((), ())),
                                preferred_element_type=F32) * scale
            s = jnp.where(valid, s, -jnp.inf)
            sink = sink_ref[li, h * G + g]
            m = jnp.maximum(jnp.max(s, axis=-1, keepdims=True), sink)
            p = jnp.exp(s - m)
            denom = jnp.sum(p, axis=-1, keepdims=True) + jnp.exp(sink - m)
            outs.append(jnp.dot(p.astype(BF16), vh, preferred_element_type=F32) / denom)
        o_ref[:, h * G * d:(h + 1) * G * d] = jnp.concatenate(outs, axis=-1).astype(o_ref.dtype)


def sliding_window_attention(proj, sinks, li):
    S = proj.shape[0]
    T = BLOCK
    qw = SWA_HEADS * SWA_HD
    kw = SWA_KV * SWA_HD
    qb = 2 * SGU_CH // qw
    kb = (2 * SGU_CH + qw) // kw
    vb = kb + 1
    prev = lambda i: jnp.maximum(i - 1, 0)
    return pl.pallas_call(
        functools.partial(_swa_kernel, li=li),
        grid=(S // T,),
        in_specs=[pl.BlockSpec(memory_space=pltpu.SMEM),
                  pl.BlockSpec((T, qw), lambda i: (i, qb)),
                  pl.BlockSpec((T, kw), lambda i: (prev(i), kb)),
                  pl.BlockSpec((T, kw), lambda i: (i, kb)),
                  pl.BlockSpec((T, kw), lambda i: (prev(i), vb)),
                  pl.BlockSpec((T, kw), lambda i: (i, vb))],
        out_specs=pl.BlockSpec((T, qw), lambda i: (i, 0)),
        out_shape=jax.ShapeDtypeStruct((S, qw), BF16),
        compiler_params=_params(("arbitrary",)),
        name="sliding_window_attention",
    )(sinks, proj, proj, proj, proj, proj)


def _xattn_kernel(q_ref, kv_ref, o_ref):
    d = XA_HD
    c_exp = (d ** -0.5) * LOG2E
    for h in range(XA_HEADS):
        q = q_ref[:, h * d:(h + 1) * d]
        k = kv_ref[:, h * d:(h + 1) * d]
        v = kv_ref[:, (XA_HEADS + h) * d:(XA_HEADS + h + 1) * d]
        s = lax.dot_general(q, k, (((1,), (1,)), ((), ())), preferred_element_type=F32)
        m = jnp.max(s, axis=-1, keepdims=True)
        p = jnp.exp2((s - m) * c_exp)
        l = jnp.sum(p, axis=-1, keepdims=True)
        o = jnp.dot(p.astype(BF16), v, preferred_element_type=F32) / l
        o_ref[:, h * d:(h + 1) * d] = o.astype(o_ref.dtype)


def cross_attention(q, kv, tq=512):
    S, qw = q.shape
    M = kv.shape[0]
    return pl.pallas_call(
        _xattn_kernel,
        grid=(S // tq,),
        in_specs=[pl.BlockSpec((tq, qw), lambda i: (i, 0)),
                  pl.BlockSpec((M, 2 * qw), lambda i: (0, 0))],
        out_specs=pl.BlockSpec((tq, qw), lambda i: (i, 0)),
        out_shape=jax.ShapeDtypeStruct((S, qw), BF16),
        compiler_params=_params(("arbitrary",)),
        name="cross_attention",
    )(q, kv)


def _router_kernel(h_ref, g_ref, wr_ref, br_ref, hn_ref, route_ref):
    x = h_ref[...]
    ms = jnp.mean(x * x, axis=-1, keepdims=True)
    hn = x * lax.rsqrt(ms + EPS) * g_ref[...]
    hn_ref[...] = hn.astype(hn_ref.dtype)
    logits = jnp.dot(hn, wr_ref[...], preferred_element_type=F32,
                     precision=lax.Precision.HIGHEST) + br_ref[...]
    tm = x.shape[0]
    lane = lax.broadcasted_iota(jnp.int32, (tm, LANES), 1)
    big = jnp.int32(2 * LANES)
    gl = jnp.where(lane < N_GROUPS, logits, -jnp.inf)
    gmax = jnp.max(gl, axis=-1, keepdims=True)
    gsel = jnp.min(jnp.where(gl == gmax, lane, big), axis=-1, keepdims=True)
    g_w = 1.0 / jnp.sum(jnp.exp(gl - gmax), axis=-1, keepdims=True)
    lo = N_GROUPS + gsel * EXP_PER_GROUP
    el = jnp.where((lane >= lo) & (lane < lo + EXP_PER_GROUP), logits, -jnp.inf)
    e1 = jnp.max(el, axis=-1, keepdims=True)
    i1 = jnp.min(jnp.where(el == e1, lane, big), axis=-1, keepdims=True)
    el2 = jnp.where(lane == i1, -jnp.inf, el)
    e2 = jnp.max(el2, axis=-1, keepdims=True)
    i2 = jnp.min(jnp.where(el2 == e2, lane, big), axis=-1, keepdims=True)
    t = jnp.exp(e2 - e1)
    w1 = g_w / (1.0 + t)
    w2 = g_w * t / (1.0 + t)
    id1 = (i1 - N_GROUPS).astype(F32)
    id2 = (i2 - N_GROUPS).astype(F32)
    route = jnp.where(lane == 0, id1, jnp.where(lane == 1, id2,
                      jnp.where(lane == 2, w1, jnp.where(lane == 3, w2, 0.0))))
    route_ref[...] = route


def moe_router(h, g3, wr, br, layer, tm=256):
    S, D = h.shape
    return pl.pallas_call(
        _router_kernel,
        grid=(S // tm,),
        in_specs=[pl.BlockSpec((tm, D), lambda i: (i, 0)),
                  pl.BlockSpec((None, 1, D), lambda i: (layer, 0, 0)),
                  pl.BlockSpec((D, LANES), lambda i: (0, 0)),
                  pl.BlockSpec((1, LANES), lambda i: (0, 0))],
        out_specs=[pl.BlockSpec((tm, D), lambda i: (i, 0)),
                   pl.BlockSpec((tm, LANES), lambda i: (i, 0))],
        out_shape=[jax.ShapeDtypeStruct((S, D), BF16),
                   jax.ShapeDtypeStruct((S, LANES), F32)],
        compiler_params=_params(("arbitrary",), VMEM_LIMIT),
        name="moe_router",
    )(h, g3, wr, br)


def _expert_kernel(be_ref, nu_ref, x_ref, wg_ref, wu_ref, wd_ref, o_ref):
    b = pl.program_id(0)

    @pl.when(b < nu_ref[0])
    def _():
        x = x_ref[...]
        hg = jnp.dot(x, wg_ref[...].astype(BF16), preferred_element_type=F32)
        hu = jnp.dot(x, wu_ref[...].astype(BF16), preferred_element_type=F32)
        hb = (hg * jax.nn.sigmoid(hg) * hu).astype(BF16)
        o_ref[...] = jnp.dot(hb, wd_ref[...].astype(BF16),
                             preferred_element_type=F32).astype(o_ref.dtype)

    @pl.when(b >= nu_ref[0])
    def _():
        o_ref[...] = jnp.zeros_like(o_ref)


def moe_experts(buf, blk_e, n_used, w_gate, w_up, w_down, layer):
    R, D = buf.shape
    nblk = R // MOE_BLOCK
    grid_spec = pltpu.PrefetchScalarGridSpec(
        num_scalar_prefetch=2,
        grid=(nblk,),
        in_specs=[pl.BlockSpec((MOE_BLOCK, D), lambda b, be, nu: (b, 0)),
                  pl.BlockSpec((None, None, D, D_FF), lambda b, be, nu: (layer, be[b], 0, 0)),
                  pl.BlockSpec((None, None, D, D_FF), lambda b, be, nu: (layer, be[b], 0, 0)),
                  pl.BlockSpec((None, None, D_FF, D), lambda b, be, nu: (layer, be[b], 0, 0))],
        out_specs=pl.BlockSpec((MOE_BLOCK, D), lambda b, be, nu: (b, 0)),
    )
    return pl.pallas_call(
        _expert_kernel,
        grid_spec=grid_spec,
        out_shape=jax.ShapeDtypeStruct((R, D), BF16),
        compiler_params=_params(("arbitrary",), VMEM_LIMIT),
        name="moe_experts",
    )(blk_e, n_used, buf, w_gate, w_up, w_down)


def hierarchical_moe(h, g3, wr_g, br_g, wr_e, br_e, w_gate, w_up, w_down, layer):
    S, D = h.shape
    wr = jnp.concatenate([wr_g[layer], jnp.transpose(wr_e[layer], (1, 0, 2)).reshape(D, N_EXPERTS)],
                         axis=1)
    wr = jnp.pad(wr, ((0, 0), (0, LANES - wr.shape[1])))
    br = jnp.concatenate([br_g[layer], br_e[layer].reshape(-1)])
    br = jnp.pad(br, (0, LANES - br.shape[0])).reshape(1, LANES)
    hn, route = moe_router(h, g3, wr, br, layer)

    expert_id = route[:, 0:TOP_K].astype(jnp.int32)
    top_w = route[:, TOP_K:2 * TOP_K]
    A = S * TOP_K
    flat_e = expert_id.reshape(-1)
    flat_tok = jnp.repeat(jnp.arange(S, dtype=jnp.int32), TOP_K)
    flat_w = top_w.reshape(-1)
    order = jnp.argsort(flat_e)
    se, stok, sw = flat_e[order], flat_tok[order], flat_w[order]
    counts = jnp.zeros((N_EXPERTS,), jnp.int32).at[flat_e].add(1)
    start = jnp.cumsum(counts) - counts
    padded = ((counts + MOE_BLOCK - 1) // MOE_BLOCK) * MOE_BLOCK
    pend = jnp.cumsum(padded)
    pstart = pend - padded
    pos = pstart[se] + (jnp.arange(A, dtype=jnp.int32) - start[se])
    nblk = -(-A // MOE_BLOCK) + N_EXPERTS
    buf = jnp.zeros((nblk * MOE_BLOCK, D), hn.dtype).at[pos].set(hn[stok])
    blk_e = jnp.minimum(jnp.searchsorted(pend, jnp.arange(nblk, dtype=jnp.int32) * MOE_BLOCK,
                                         side='right'), N_EXPERTS - 1).astype(jnp.int32)
    n_used = (pend[-1] // MOE_BLOCK).astype(jnp.int32).reshape(1)
    yb = moe_experts(buf, blk_e, n_used, w_gate, w_up, w_down, layer)
    ys = yb[pos].astype(F32) * sw[:, None]
    return jax.ops.segment_sum(ys, stok, num_segments=S)


def kernel(x, mem, mix_norm, even_w_in, conv_w, conv_b, conv_norm_g, conv_norm_b, diff_lambda, diff_subln_g, even_w_out, odd_w_in, sgu_norm_g, sgu_norm_b, sgu_w, sgu_b, swa_sinks, odd_w_out, xa_norm, mem_norm, xa_wq, xa_wkv, xa_wo, moe_norm, router_group_w, router_group_b, router_expert_w, router_expert_b, w_gate, w_up, w_down, final_norm):
    B, S, D = x.shape
    assert B == 1
    row3 = lambda p: p.reshape(p.shape[0], 1, p.shape[1])
    h = x.reshape(S, D)
    mem2 = mem.reshape(mem.shape[1], D)
    for layer in range(DEPTH):
        i = layer // 2
        hn = rmsnorm(h, row3(mix_norm), layer, BF16)
        if layer % 2 == 0:
            proj = matmul([hn], even_w_in, i, out_dtype=BF16, tm=512, tn=512)
            y_a = conformer_conv(proj, conv_w, row3(conv_b), row3(conv_norm_g), row3(conv_norm_b), i)
            lambda_init = 0.8 - 0.6 * math.exp(-0.3 * layer)
            y_b = diff_attention(proj, diff_lambda, row3(diff_subln_g), i, lambda_init)
            h = matmul([y_a, y_b], even_w_out, i, res=h, out_dtype=F32, tm=512, tn=512)
        else:
            proj = matmul([hn], odd_w_in, i, out_dtype=BF16, tm=512, tn=512)
            y_c = spatial_gating(proj, row3(sgu_norm_g), row3(sgu_norm_b), sgu_w,
                                 jnp.transpose(sgu_b, (0, 2, 1)), i)
            y_d = sliding_window_attention(proj, swa_sinks, i)
            h = matmul([y_c, y_d], odd_w_out, i, res=h, out_dtype=F32, tm=512, tn=512)
        hq = rmsnorm(h, row3(xa_norm), layer, BF16)
        mem_n = rmsnorm(mem2, row3(mem_norm), layer, BF16)
        q = matmul([hq], xa_wq, layer, out_dtype=BF16, tm=512, tn=512)
        kv = matmul([mem_n], xa_wkv, layer, out_dtype=BF16, tm=mem2.shape[0], tn=512)
        o = cross_attention(q, kv)
        h = matmul([o], xa_wo, layer, res=h, out_dtype=F32, tm=512, tn=512)
        h = h + hierarchical_moe(h, row3(moe_norm), router_group_w, router_group_b,
                                 router_expert_w, router_expert_b, w_gate, w_up, w_down, layer)
    out = rmsnorm(h, final_norm.reshape(1, 1, D), 0, F32)
    return out.reshape(B, S, D)
```

```python
import functools
import math

import jax
import jax.numpy as jnp
from jax import lax
from jax.experimental import pallas as pl
from jax.experimental.pallas import tpu as pltpu

F32 = jnp.float32
BF16 = jnp.bfloat16

D_MODEL = 4096
DEPTH = 2
EPS = 1e-5
BLOCK = 128

CONV_CH = 2048
CONV_WIDTH = 31
CONV_GROUPS = 16
CONV_HALO = 32

DIFF_HEADS = 8
DIFF_HD = 128
DIFF_QK = DIFF_HEADS * 2 * DIFF_HD
DIFF_V = DIFF_HEADS * 2 * DIFF_HD

SGU_CH = 2048
SGU_GROUPS = 8
SGU_CHUNK = 128

SWA_HEADS = 32
SWA_KV = 4
SWA_HD = 64
WINDOW = 128

XA_HEADS = 4
XA_HD = 256

N_GROUPS = 8
EXP_PER_GROUP = 8
N_EXPERTS = N_GROUPS * EXP_PER_GROUP
TOP_K = 2
D_FF = 384
MOE_BLOCK = 128

LANES = 128
LOG2E = math.log2(math.e)
VMEM_LIMIT = 56 * 1024 * 1024


def _params(sem, vmem=None):
    return pltpu.CompilerParams(dimension_semantics=sem, vmem_limit_bytes=vmem)


def _rmsnorm_kernel(x_ref, g_ref, o_ref):
    x = x_ref[...].astype(F32)
    ms = jnp.mean(x * x, axis=-1, keepdims=True)
    o_ref[...] = (x * lax.rsqrt(ms + EPS) * g_ref[...]).astype(o_ref.dtype)


def rmsnorm(x, g2, layer, out_dtype, tm=256):
    M, D = x.shape
    return pl.pallas_call(
        _rmsnorm_kernel,
        grid=(M // tm,),
        in_specs=[pl.BlockSpec((tm, D), lambda i: (i, 0)),
                  pl.BlockSpec((None, 1, D), lambda i: (layer, 0, 0))],
        out_specs=pl.BlockSpec((tm, D), lambda i: (i, 0)),
        out_shape=jax.ShapeDtypeStruct((M, D), out_dtype),
        compiler_params=_params(("arbitrary",)),
        name="rmsnorm",
    )(x, g2)


def _mm_kernel(*refs, k_splits, has_res):
    n_a = len(k_splits)
    a_refs = refs[:n_a]
    w_ref = refs[n_a]
    r_ref = refs[n_a + 1] if has_res else None
    o_ref = refs[n_a + 1 + has_res]
    wb_ref = refs[n_a + 2 + has_res]

    @pl.when(pl.program_id(1) == 0)
    def _():
        wb_ref[...] = w_ref[...].astype(BF16)

    acc = None
    off = 0
    for a_ref, ks in zip(a_refs, k_splits):
        part = jnp.dot(a_ref[...], wb_ref[off:off + ks, :], preferred_element_type=F32)
        acc = part if acc is None else acc + part
        off += ks
    if has_res:
        acc = acc + r_ref[...]
    o_ref[...] = acc.astype(o_ref.dtype)


def matmul(a_list, w3, layer, *, res=None, out_dtype, tm, tn):
    M = a_list[0].shape[0]
    _, K, N = w3.shape
    k_splits = tuple(a.shape[1] for a in a_list)
    assert sum(k_splits) == K and M % tm == 0 and N % tn == 0
    in_specs = [pl.BlockSpec((tm, ks), lambda j, i: (i, 0)) for ks in k_splits]
    in_specs.append(pl.BlockSpec((None, K, tn), lambda j, i: (layer, 0, j)))
    args = list(a_list) + [w3]
    if res is not None:
        in_specs.append(pl.BlockSpec((tm, tn), lambda j, i: (i, j)))
        args.append(res)
    return pl.pallas_call(
        functools.partial(_mm_kernel, k_splits=k_splits, has_res=res is not None),
        grid=(N // tn, M // tm),
        in_specs=in_specs,
        out_specs=pl.BlockSpec((tm, tn), lambda j, i: (i, j)),
        out_shape=jax.ShapeDtypeStruct((M, N), out_dtype),
        scratch_shapes=[pltpu.VMEM((K, tn), BF16)],
        compiler_params=_params(("arbitrary", "arbitrary"), VMEM_LIMIT),
        name="matmul",
    )(*args)


def _conv_kernel(a_ref, g_ref, w_ref, b_ref, ng_ref, nb_ref, o_ref, ybuf, *, T, LB):
    i = pl.program_id(1)

    @pl.when(i == 0)
    def _():
        ybuf[0:CONV_HALO, :] = jnp.zeros((CONV_HALO, LB), F32)

    @pl.when(i > 0)
    def _():
        ybuf[0:CONV_HALO, :] = ybuf[T:T + CONV_HALO, :]

    a = a_ref[...].astype(F32)
    g = g_ref[...].astype(F32)
    ybuf[CONV_HALO:CONV_HALO + T, :] = a * jax.nn.sigmoid(g)

    first = CONV_HALO - (CONV_WIDTH - 1)
    for gi in range(LB // LANES):
        cols = slice(gi * LANES, (gi + 1) * LANES)
        acc = jnp.broadcast_to(b_ref[:, cols], (T, LANES))
        for j in range(CONV_WIDTH):
            acc = acc + w_ref[j:j + 1, cols] * ybuf[first + j:first + j + T, cols]
        mu = jnp.mean(acc, axis=-1, keepdims=True)
        d = acc - mu
        var = jnp.mean(d * d, axis=-1, keepdims=True)
        y = d * lax.rsqrt(var + EPS) * ng_ref[:, cols] + nb_ref[:, cols]
        o_ref[:, cols] = (y * jax.nn.sigmoid(y)).astype(o_ref.dtype)


def conformer_conv(proj, conv_w, conv_b, norm_g, norm_b, li, T=256, LB=512):
    S = proj.shape[0]
    assert CONV_CH // CONV_GROUPS == LANES
    ncb = CONV_CH // LB
    vec = lambda: pl.BlockSpec((None, 1, LB), lambda c, i: (li, 0, c))
    return pl.pallas_call(
        functools.partial(_conv_kernel, T=T, LB=LB),
        grid=(ncb, S // T),
        in_specs=[pl.BlockSpec((T, LB), lambda c, i: (i, c)),
                  pl.BlockSpec((T, LB), lambda c, i: (i, ncb + c)),
                  pl.BlockSpec((None, CONV_WIDTH, LB), lambda c, i: (li, 0, c)),
                  vec(), vec(), vec()],
        out_specs=pl.BlockSpec((T, LB), lambda c, i: (i, c)),
        out_shape=jax.ShapeDtypeStruct((S, CONV_CH), BF16),
        scratch_shapes=[pltpu.VMEM((CONV_HALO + T, LB), F32)],
        compiler_params=_params(("arbitrary", "arbitrary")),
        name="conformer_conv",
    )(proj, proj, conv_w, conv_b, norm_g, norm_b)


def _diffattn_kernel(q_ref, k_ref, v_ref, lam_ref, g_ref, o_ref, acc_ref, m_ref, l_ref, s0_ref, s1_ref,
                     *, tq, lambda_init):
    qi = pl.program_id(1)
    d = DIFF_HD
    tk = 2 * tq
    c_exp = (d ** -0.5) * LOG2E
    nfull = qi // 2
    acc_ref[...] = jnp.zeros_like(acc_ref)
    m_ref[...] = jnp.full_like(m_ref, -jnp.inf)
    l_ref[...] = jnp.zeros_like(l_ref)

    def qk(t, s_ref):
        kblk = k_ref[pl.ds(pl.multiple_of(t * tk, tk), tk), :]
        for c in range(2):
            s_ref[c] = lax.dot_general(q_ref[:, c * d:(c + 1) * d], kblk[:, c * d:(c + 1) * d],
                                       (((1,), (1,)), ((), ())), preferred_element_type=F32)

    def softmax_pv(t, s_ref, width, masked, row0):
        vblk = v_ref[pl.ds(pl.multiple_of(t * tk, tk), width), :]
        for c in range(2):
            s = s_ref[c, :, 0:width]
            if masked:
                row = lax.broadcasted_iota(jnp.int32, (tq, width), 0) + row0
                col = lax.broadcasted_iota(jnp.int32, (tq, width), 1)
                s = jnp.where(row >= col, s, -jnp.inf)
            m = m_ref[c]
            m_new = jnp.maximum(m, jnp.max(s, axis=-1, keepdims=True))
            alpha = jnp.exp2((m - m_new) * c_exp)
            p = jnp.exp2((s - m_new) * c_exp)
            l_ref[c] = alpha * l_ref[c] + jnp.sum(p, axis=-1, keepdims=True)
            m_ref[c] = m_new
            acc_ref[c] = alpha * acc_ref[c] + jnp.dot(p.astype(BF16), vblk,
                                                      preferred_element_type=F32)

    @pl.when(nfull % 2 == 0)
    def _():
        qk(0, s0_ref)

    @pl.when(nfull % 2 == 1)
    def _():
        qk(0, s1_ref)

    def body(t, carry):
        par = (nfull - t) % 2

        @pl.when(par == 0)
        def _():
            qk(t + 1, s1_ref)
            softmax_pv(t, s0_ref, tk, False, 0)

        @pl.when(par == 1)
        def _():
            qk(t + 1, s0_ref)
            softmax_pv(t, s1_ref, tk, False, 0)
        return carry

    lax.fori_loop(0, nfull, body, 0)

    @pl.when(qi % 2 == 0)
    def _():
        softmax_pv(nfull, s0_ref, tq, True, 0)

    @pl.when(qi % 2 == 1)
    def _():
        softmax_pv(nfull, s0_ref, tk, True, tq)

    lv = lam_ref[...]
    lam = (jnp.exp(jnp.sum(lv[0:1] * lv[1:2], axis=-1, keepdims=True))
           - jnp.exp(jnp.sum(lv[2:3] * lv[3:4], axis=-1, keepdims=True)) + lambda_init)
    o = acc_ref[0] / l_ref[0] - lam * (acc_ref[1] / l_ref[1])
    ms = jnp.mean(o * o, axis=-1, keepdims=True)
    o = o * lax.rsqrt(ms + EPS) * g_ref[...] * (1.0 - lambda_init)
    o_ref[...] = o.astype(o_ref.dtype)


def diff_attention(proj, lam_vecs, subln_g3, li, lambda_init, tq=512):
    S = proj.shape[0]
    assert S % (2 * tq) == 0
    hw = 2 * DIFF_HD
    q0 = 2 * CONV_CH // hw
    k0 = q0 + DIFF_QK // hw
    v0 = k0 + DIFF_QK // hw
    return pl.pallas_call(
        functools.partial(_diffattn_kernel, tq=tq, lambda_init=lambda_init),
        grid=(DIFF_HEADS, S // tq),
        in_specs=[pl.BlockSpec((tq, hw), lambda h, i: (i, q0 + h)),
                  pl.BlockSpec((S, hw), lambda h, i: (0, k0 + h)),
                  pl.BlockSpec((S, hw), lambda h, i: (0, v0 + h)),
                  pl.BlockSpec((None, 4, DIFF_HD), lambda h, i: (li, 0, 0)),
                  pl.BlockSpec((None, 1, hw), lambda h, i: (li, 0, 0))],
        out_specs=pl.BlockSpec((tq, hw), lambda h, i: (i, h)),
        out_shape=jax.ShapeDtypeStruct((S, DIFF_V), BF16),
        scratch_shapes=[pltpu.VMEM((2, tq, hw), F32),
                        pltpu.VMEM((2, tq, 1), F32),
                        pltpu.VMEM((2, tq, 1), F32),
                        pltpu.VMEM((2, tq, 2 * tq), F32),
                        pltpu.VMEM((2, tq, 2 * tq), F32)],
        compiler_params=_params(("arbitrary", "arbitrary"), VMEM_LIMIT),
        name="diff_attention",
    )(proj, proj, proj, lam_vecs, subln_g3)


def _sgu_kernel(u_ref, v_ref, ng_ref, nb_ref, w_ref, bs_ref, o_ref):
    T = SGU_CHUNK
    gw = SGU_CH // SGU_GROUPS
    row = lax.broadcasted_iota(jnp.int32, (T, T), 0)
    col = lax.broadcasted_iota(jnp.int32, (T, T), 1)
    causal = row >= col
    for g in range(SGU_GROUPS):
        cols = slice(g * gw, (g + 1) * gw)
        u = jax.nn.gelu(u_ref[:, cols].astype(F32), approximate=True)
        v = jax.nn.gelu(v_ref[:, cols].astype(F32), approximate=True)
        mu = jnp.mean(v, axis=-1, keepdims=True)
        d = v - mu
        var = jnp.mean(d * d, axis=-1, keepdims=True)
        vn = d * lax.rsqrt(var + EPS) * ng_ref[:, cols] + nb_ref[:, cols]
        w = jnp.where(causal, w_ref[g], 0.0).astype(BF16)
        mixed = jnp.dot(w, vn.astype(BF16), preferred_element_type=F32) + bs_ref[:, g:g + 1]
        o_ref[:, cols] = (u * mixed).astype(o_ref.dtype)


def spatial_gating(proj, norm_g3, norm_b3, w_s, b_s_t, li):
    S = proj.shape[0]
    T = SGU_CHUNK
    vec = lambda: pl.BlockSpec((None, 1, SGU_CH), lambda i: (li, 0, 0))
    return pl.pallas_call(
        _sgu_kernel,
        grid=(S // T,),
        in_specs=[pl.BlockSpec((T, SGU_CH), lambda i: (i, 0)),
                  pl.BlockSpec((T, SGU_CH), lambda i: (i, 1)),
                  vec(), vec(),
                  pl.BlockSpec((None, SGU_GROUPS, T, T), lambda i: (li, 0, 0, 0)),
                  pl.BlockSpec((None, T, SGU_GROUPS), lambda i: (li, 0, 0))],
        out_specs=pl.BlockSpec((T, SGU_CH), lambda i: (i, 0)),
        out_shape=jax.ShapeDtypeStruct((S, SGU_CH), BF16),
        compiler_params=_params(("arbitrary",)),
        name="spatial_gating",
    )(proj, proj, norm_g3, norm_b3, w_s, b_s_t)


def _swa_kernel(sink_ref, q_ref, kp_ref, kc_ref, vp_ref, vc_ref, o_ref, *, li):
    n = pl.program_id(0)
    T = BLOCK
    G = SWA_HEADS // SWA_KV
    d = SWA_HD
    kk = jnp.concatenate([kp_ref[...], kc_ref[...]], axis=0)
    vv = jnp.concatenate([vp_ref[...], vc_ref[...]], axis=0)
    qi = lax.broadcasted_iota(jnp.int32, (T, 2 * T), 0)
    kj = lax.broadcasted_iota(jnp.int32, (T, 2 * T), 1)
    rel = qi + T - kj
    valid = (rel >= 0) & (rel < WINDOW) & ((n * T + kj - T) >= 0)
    scale = d ** -0.5
    for h in range(SWA_KV):
        kh = kk[:, h * d:(h + 1) * d]
        vh = vv[:, h * d:(h + 1) * d]
        outs = []
        for g in range(G):
            c0 = (h * G + g) * d
            s = lax.dot_general(q_ref[:, c0:c0 + d], kh, (((1,), (1,)), ((), ())),
                                preferred_element_type=F32) * scale
            s = jnp.where(valid, s, -jnp.inf)
            sink = sink_ref[li, h * G + g]
            m = jnp.maximum(jnp.max(s, axis=-1, keepdims=True), sink)
            p = jnp.exp(s - m)
            denom = jnp.sum(p, axis=-1, keepdims=True) + jnp.exp(sink - m)
            outs.append(jnp.dot(p.astype(BF16), vh, preferred_element_type=F32) / denom)
        o_ref[:, h * G * d:(h + 1) * G * d] = jnp.concatenate(outs, axis=-1).astype(o_ref.dtype)


def sliding_window_attention(proj, sinks, li):
    S = proj.shape[0]
    T = BLOCK
    qw = SWA_HEADS * SWA_HD
    kw = SWA_KV * SWA_HD
    qb = 2 * SGU_CH // qw
    kb = (2 * SGU_CH + qw) // kw
    vb = kb + 1
    prev = lambda i: jnp.maximum(i - 1, 0)
    return pl.pallas_call(
        functools.partial(_swa_kernel, li=li),
        grid=(S // T,),
        in_specs=[pl.BlockSpec(memory_space=pltpu.SMEM),
                  pl.BlockSpec((T, qw), lambda i: (i, qb)),
                  pl.BlockSpec((T, kw), lambda i: (prev(i), kb)),
                  pl.BlockSpec((T, kw), lambda i: (i, kb)),
                  pl.BlockSpec((T, kw), lambda i: (prev(i), vb)),
                  pl.BlockSpec((T, kw), lambda i: (i, vb))],
        out_specs=pl.BlockSpec((T, qw), lambda i: (i, 0)),
        out_shape=jax.ShapeDtypeStruct((S, qw), BF16),
        compiler_params=_params(("arbitrary",)),
        name="sliding_window_attention",
    )(sinks, proj, proj, proj, proj, proj)


def _xattn_kernel(q_ref, kv_ref, o_ref):
    d = XA_HD
    c_exp = (d ** -0.5) * LOG2E
    for h in range(XA_HEADS):
        q = q_ref[:, h * d:(h + 1) * d]
        k = kv_ref[:, h * d:(h + 1) * d]
        v = kv_ref[:, (XA_HEADS + h) * d:(XA_HEADS + h + 1) * d]
        s = lax.dot_general(q, k, (((1,), (1,)), ((), ())), preferred_element_type=F32)
        m = jnp.max(s, axis=-1, keepdims=True)
        p = jnp.exp2((s - m) * c_exp)
        l = jnp.sum(p, axis=-1, keepdims=True)
        o = jnp.dot(p.astype(BF16), v, preferred_element_type=F32) / l
        o_ref[:, h * d:(h + 1) * d] = o.astype(o_ref.dtype)


def cross_attention(q, kv, tq=512):
    S, qw = q.shape
    M = kv.shape[0]
    return pl.pallas_call(
        _xattn_kernel,
        grid=(S // tq,),
        in_specs=[pl.BlockSpec((tq, qw), lambda i: (i, 0)),
                  pl.BlockSpec((M, 2 * qw), lambda i: (0, 0))],
        out_specs=pl.BlockSpec((tq, qw), lambda i: (i, 0)),
        out_shape=jax.ShapeDtypeStruct((S, qw), BF16),
        compiler_params=_params(("arbitrary",)),
        name="cross_attention",
    )(q, kv)


def _pack_bf16_pair(y):
    c = y.shape[1] // 2
    hi = lax.bitcast_convert_type(y[:, :c].astype(BF16).astype(F32), jnp.uint32)
    lo = lax.bitcast_convert_type(y[:, c:].astype(BF16).astype(F32), jnp.uint32)
    return hi | lax.shift_right_logical(lo, jnp.uint32(16))


def _unpack_bf16_pair(u):
    hi = lax.bitcast_convert_type(u & jnp.uint32(0xFFFF0000), F32)
    lo = lax.bitcast_convert_type(lax.shift_left(u, jnp.uint32(16)), F32)
    return hi, lo


def _router_kernel(h_ref, g_ref, wr_ref, br_ref, hn_ref, route_ref):
    x = h_ref[...]
    ms = jnp.mean(x * x, axis=-1, keepdims=True)
    hn = x * lax.rsqrt(ms + EPS) * g_ref[...]
    hn_ref[...] = _pack_bf16_pair(hn)
    logits = jnp.dot(hn, wr_ref[...], preferred_element_type=F32,
                     precision=lax.Precision.HIGHEST) + br_ref[...]
    tm = x.shape[0]
    lane = lax.broadcasted_iota(jnp.int32, (tm, LANES), 1)
    big = jnp.int32(2 * LANES)
    gl = jnp.where(lane < N_GROUPS, logits, -jnp.inf)
    gmax = jnp.max(gl, axis=-1, keepdims=True)
    gsel = jnp.min(jnp.where(gl == gmax, lane, big), axis=-1, keepdims=True)
    g_w = 1.0 / jnp.sum(jnp.exp(gl - gmax), axis=-1, keepdims=True)
    lo = N_GROUPS + gsel * EXP_PER_GROUP
    el = jnp.where((lane >= lo) & (lane < lo + EXP_PER_GROUP), logits, -jnp.inf)
    e1 = jnp.max(el, axis=-1, keepdims=True)
    i1 = jnp.min(jnp.where(el == e1, lane, big), axis=-1, keepdims=True)
    el2 = jnp.where(lane == i1, -jnp.inf, el)
    e2 = jnp.max(el2, axis=-1, keepdims=True)
    i2 = jnp.min(jnp.where(el2 == e2, lane, big), axis=-1, keepdims=True)
    t = jnp.exp(e2 - e1)
    w1 = g_w / (1.0 + t)
    w2 = g_w * t / (1.0 + t)
    id1 = (i1 - N_GROUPS).astype(F32)
    id2 = (i2 - N_GROUPS).astype(F32)
    route = jnp.where(lane == 0, id1, jnp.where(lane == 1, id2,
                      jnp.where(lane == 2, w1, jnp.where(lane == 3, w2, 0.0))))
    route_ref[...] = route


def moe_router(h, g3, wr, br, layer, tm=256):
    S, D = h.shape
    return pl.pallas_call(
        _router_kernel,
        grid=(S // tm,),
        in_specs=[pl.BlockSpec((tm, D), lambda i: (i, 0)),
                  pl.BlockSpec((None, 1, D), lambda i: (layer, 0, 0)),
                  pl.BlockSpec((D, LANES), lambda i: (0, 0)),
                  pl.BlockSpec((1, LANES), lambda i: (0, 0))],
        out_specs=[pl.BlockSpec((tm, D // 2), lambda i: (i, 0)),
                   pl.BlockSpec((tm, LANES), lambda i: (i, 0))],
        out_shape=[jax.ShapeDtypeStruct((S, D // 2), jnp.uint32),
                   jax.ShapeDtypeStruct((S, LANES), F32)],
        compiler_params=_params(("arbitrary",), VMEM_LIMIT),
        name="moe_router",
    )(h, g3, wr, br)


def _expert_kernel(be_ref, first_ref, slot_ref, nexte_ref, nu_ref,
                   tok0_ref, tokn_ref,
                   hn_hbm, wg_hbm, wu_hbm, wd_hbm,
                   o_ref, xbuf, wgbuf, wubuf, wdbuf, xsem, wsem, *, layer):
    b = pl.program_id(0)
    nu = nu_ref[0]
    xs = b % 2

    def w_copies(e, s):
        return (pltpu.make_async_copy(wg_hbm.at[layer, e], wgbuf.at[s], wsem.at[s, 0]),
                pltpu.make_async_copy(wu_hbm.at[layer, e], wubuf.at[s], wsem.at[s, 1]),
                pltpu.make_async_copy(wd_hbm.at[layer, e], wdbuf.at[s], wsem.at[s, 2]))

    def row_copy(t, s, r):
        return pltpu.make_async_copy(hn_hbm.at[pl.ds(t, 1), :], xbuf.at[s, pl.ds(r, 1), :], xsem.at[s])

    def start_rows(tok_ref, s):
        def body(r, c):
            row_copy(tok_ref[0, r], s, r).start()
            return c
        lax.fori_loop(0, MOE_BLOCK, body, 0, unroll=8)

    def wait_rows(s):
        def body(r, c):
            row_copy(0, s, r).wait()
            return c
        lax.fori_loop(0, MOE_BLOCK, body, 0, unroll=8)

    @pl.when(b == 0)
    def _():
        start_rows(tok0_ref, 0)
        for c in w_copies(be_ref[0], 0):
            c.start()

    @pl.when(b + 1 < nu)
    def _():
        start_rows(tokn_ref, 1 - xs)

    @pl.when(b < nu)
    def _():
        s = slot_ref[b]

        @pl.when(first_ref[b] == 1)
        def _():
            for c in w_copies(be_ref[b], s):
                c.wait()
            ne = nexte_ref[b]

            @pl.when(ne >= 0)
            def _():
                for c in w_copies(ne, 1 - s):
                    c.start()

        wait_rows(xs)
        x_hi, x_lo = _unpack_bf16_pair(xbuf[xs])
        half = x_hi.shape[1]
        hg = (jnp.dot(x_hi, wgbuf[s, 0:half, :], preferred_element_type=F32)
              + jnp.dot(x_lo, wgbuf[s, half:, :], preferred_element_type=F32))
        hu = (jnp.dot(x_hi, wubuf[s, 0:half, :], preferred_element_type=F32)
              + jnp.dot(x_lo, wubuf[s, half:, :], preferred_element_type=F32))
        hb = (hg * jax.nn.sigmoid(hg) * hu).astype(BF16).astype(F32)
        o_ref[...] = _pack_bf16_pair(jnp.dot(hb, wdbuf[s], preferred_element_type=F32))

    @pl.when(b >= nu)
    def _():
        o_ref[...] = jnp.zeros_like(o_ref)


def moe_experts(hn_packed, plan, w_gate, w_up, w_down, layer):
    S, C = hn_packed.shape
    D = 2 * C
    nblk = plan["row_tok"].shape[0]
    anyspec = pl.BlockSpec(memory_space=pl.ANY)
    grid_spec = pltpu.PrefetchScalarGridSpec(
        num_scalar_prefetch=5,
        grid=(nblk,),
        in_specs=[pl.BlockSpec((None, 1, MOE_BLOCK), lambda b, *_: (0, 0, 0), memory_space=pltpu.SMEM),
                  pl.BlockSpec((None, 1, MOE_BLOCK), lambda b, *_: (jnp.minimum(b + 1, nblk - 1), 0, 0),
                               memory_space=pltpu.SMEM),
                  anyspec, anyspec, anyspec, anyspec],
        out_specs=pl.BlockSpec((MOE_BLOCK, C), lambda b, *_: (b, 0)),
        scratch_shapes=[pltpu.VMEM((2, MOE_BLOCK, C), jnp.uint32),
                        pltpu.VMEM((2, D, D_FF), F32),
                        pltpu.VMEM((2, D, D_FF), F32),
                        pltpu.VMEM((2, D_FF, D), F32),
                        pltpu.SemaphoreType.DMA((2,)),
                        pltpu.SemaphoreType.DMA((2, 3))],
    )
    return pl.pallas_call(
        functools.partial(_expert_kernel, layer=layer),
        grid_spec=grid_spec,
        out_shape=jax.ShapeDtypeStruct((nblk * MOE_BLOCK, C), jnp.uint32),
        compiler_params=_params(("arbitrary",), VMEM_LIMIT),
        name="moe_experts",
    )(plan["blk_e"], plan["first"], plan["slot"], plan["next_e"], plan["n_used"],
      plan["row_tok"], plan["row_tok"], hn_packed, w_gate, w_up, w_down)


def _combine_kernel(pos0_ref, posn_ref, h_ref, route_ref, g_ref, y_hbm, *rest, tm, nt, emit_h):
    if emit_h:
        h_out_ref, n_out_ref, ybuf, sem = rest
    else:
        n_out_ref, ybuf, sem = rest
    i = pl.program_id(0)
    xs = i % 2

    def row_copy(p, s, j):
        return pltpu.make_async_copy(y_hbm.at[pl.ds(p, 1), :], ybuf.at[s, pl.ds(j, 1), :], sem.at[s])

    def start_rows(pos_ref, s):
        def body(j, c):
            row_copy(pos_ref[0, j], s, j).start()
            return c
        lax.fori_loop(0, TOP_K * tm, body, 0, unroll=8)

    @pl.when(i == 0)
    def _():
        start_rows(pos0_ref, 0)

    @pl.when(i + 1 < nt)
    def _():
        start_rows(posn_ref, 1 - xs)

    def wait_body(j, c):
        row_copy(0, xs, j).wait()
        return c
    lax.fori_loop(0, TOP_K * tm, wait_body, 0, unroll=8)

    a_hi, a_lo = _unpack_bf16_pair(ybuf[xs, 0:tm, :])
    b_hi, b_lo = _unpack_bf16_pair(ybuf[xs, tm:2 * tm, :])
    w1 = route_ref[:, TOP_K:TOP_K + 1]
    w2 = route_ref[:, TOP_K + 1:TOP_K + 2]
    half = a_hi.shape[1]
    h_lo = h_ref[:, 0:half] + w1 * a_hi + w2 * b_hi
    h_hi = h_ref[:, half:] + w1 * a_lo + w2 * b_lo
    if emit_h:
        h_out_ref[:, 0:half] = h_lo
        h_out_ref[:, half:] = h_hi
    ms = (jnp.sum(h_lo * h_lo, axis=-1, keepdims=True)
          + jnp.sum(h_hi * h_hi, axis=-1, keepdims=True)) / (2 * half)
    r = lax.rsqrt(ms + EPS)
    n_out_ref[:, 0:half] = (h_lo * r * g_ref[:, 0:half]).astype(n_out_ref.dtype)
    n_out_ref[:, half:] = (h_hi * r * g_ref[:, half:]).astype(n_out_ref.dtype)


def moe_combine(h, route, yb, pos3, g3, g_layer, norm_dtype, emit_h, tm=128):
    S, D = h.shape
    nt = S // tm
    out_specs = [pl.BlockSpec((tm, D), lambda i: (i, 0))]
    out_shape = [jax.ShapeDtypeStruct((S, D), norm_dtype)]
    if emit_h:
        out_specs = [pl.BlockSpec((tm, D), lambda i: (i, 0))] + out_specs
        out_shape = [jax.ShapeDtypeStruct((S, D), F32)] + out_shape
    return pl.pallas_call(
        functools.partial(_combine_kernel, tm=tm, nt=nt, emit_h=emit_h),
        grid=(nt,),
        in_specs=[pl.BlockSpec((None, 1, TOP_K * tm), lambda i: (0, 0, 0), memory_space=pltpu.SMEM),
                  pl.BlockSpec((None, 1, TOP_K * tm), lambda i: (jnp.minimum(i + 1, nt - 1), 0, 0),
                               memory_space=pltpu.SMEM),
                  pl.BlockSpec((tm, D), lambda i: (i, 0)),
                  pl.BlockSpec((tm, LANES), lambda i: (i, 0)),
                  pl.BlockSpec((None, 1, D), lambda i: (g_layer, 0, 0)),
                  pl.BlockSpec(memory_space=pl.ANY)],
        out_specs=out_specs,
        out_shape=out_shape,
        scratch_shapes=[pltpu.VMEM((2, TOP_K * tm, D // 2), jnp.uint32),
                        pltpu.SemaphoreType.DMA((2,))],
        compiler_params=_params(("arbitrary",), VMEM_LIMIT),
        name="moe_combine",
    )(pos3, pos3, h, route, g3, yb)


def hierarchical_moe(h, g3, wr_g, br_g, wr_e, br_e, w_gate, w_up, w_down, layer,
                     next_g3, next_layer, next_dtype, emit_h, tm_combine=128):
    S, D = h.shape
    wr = jnp.concatenate([wr_g[layer], jnp.transpose(wr_e[layer], (1, 0, 2)).reshape(D, N_EXPERTS)],
                         axis=1)
    wr = jnp.pad(wr, ((0, 0), (0, LANES - wr.shape[1])))
    br = jnp.concatenate([br_g[layer], br_e[layer].reshape(-1)])
    br = jnp.pad(br, (0, LANES - br.shape[0])).reshape(1, LANES)
    hn_packed, route = moe_router(h, g3, wr, br, layer)
    plan = _moe_plan(route[:, 0:TOP_K].astype(jnp.int32), tm_combine)
    yb = moe_experts(hn_packed, plan, w_gate, w_up, w_down, layer)
    return moe_combine(h, route, yb, plan["pos3"], next_g3, next_layer, next_dtype, emit_h,
                       tm=tm_combine)


def _moe_plan(expert_id, tm_combine):
    S = expert_id.shape[0]
    A = S * TOP_K
    i32 = jnp.int32
    flat_e = expert_id.reshape(-1)
    order = jnp.argsort(flat_e).astype(i32)
    se = flat_e[order]
    counts = jnp.zeros((N_EXPERTS,), i32).at[flat_e].add(1)
    start = jnp.cumsum(counts) - counts
    nb_e = (counts + MOE_BLOCK - 1) // MOE_BLOCK
    bend = jnp.cumsum(nb_e)
    pstart = (bend - nb_e) * MOE_BLOCK
    pos_sorted = pstart[se] + (jnp.arange(A, dtype=i32) - start[se])
    nblk = -(-A // MOE_BLOCK) + N_EXPERTS
    row_tok = jnp.zeros((nblk * MOE_BLOCK,), i32).at[pos_sorted].set(order // TOP_K)
    pos_flat = jnp.zeros((A,), i32).at[order].set(pos_sorted)
    n_used = bend[-1]
    ar = jnp.arange(nblk, dtype=i32)
    blk_e = jnp.minimum(jnp.searchsorted(bend, ar, side='right'), N_EXPERTS - 1).astype(i32)
    first = ((ar == 0) | (blk_e != jnp.roll(blk_e, 1))) & (ar < n_used)
    slot = (jnp.cumsum(first.astype(i32)) - 1) % 2
    big = i32(nblk)
    nf = lax.cummin(jnp.where(first, ar, big)[::-1])[::-1]
    next_first = jnp.concatenate([nf[1:], big[None]])
    next_e = jnp.where(next_first < big, blk_e[jnp.minimum(next_first, nblk - 1)], -1)
    nt = S // tm_combine
    pos3 = pos_flat.reshape(nt, tm_combine, TOP_K).transpose(0, 2, 1).reshape(nt, 1, TOP_K * tm_combine)
    return dict(row_tok=row_tok.reshape(nblk, 1, MOE_BLOCK), pos3=pos3, blk_e=blk_e,
                first=first.astype(i32), slot=slot.astype(i32), next_e=next_e.astype(i32),
                n_used=n_used.astype(i32).reshape(1))


def kernel(x, mem, mix_norm, even_w_in, conv_w, conv_b, conv_norm_g, conv_norm_b, diff_lambda, diff_subln_g, even_w_out, odd_w_in, sgu_norm_g, sgu_norm_b, sgu_w, sgu_b, swa_sinks, odd_w_out, xa_norm, mem_norm, xa_wq, xa_wkv, xa_wo, moe_norm, router_group_w, router_group_b, router_expert_w, router_expert_b, w_gate, w_up, w_down, final_norm):
    B, S, D = x.shape
    assert B == 1
    row3 = lambda p: p.reshape(p.shape[0], 1, p.shape[1])
    h = x.reshape(S, D)
    mem2 = mem.reshape(mem.shape[1], D)
    hn = rmsnorm(h, row3(mix_norm), 0, BF16)
    for layer in range(DEPTH):
        i = layer // 2
        if layer % 2 == 0:
            proj = matmul([hn], even_w_in, i, out_dtype=BF16, tm=512, tn=512)
            y_a = conformer_conv(proj, conv_w, row3(conv_b), row3(conv_norm_g), row3(conv_norm_b), i)
            lambda_init = 0.8 - 0.6 * math.exp(-0.3 * layer)
            y_b = diff_attention(proj, diff_lambda, row3(diff_subln_g), i, lambda_init)
            h = matmul([y_a, y_b], even_w_out, i, res=h, out_dtype=F32, tm=512, tn=512)
        else:
            proj = matmul([hn], odd_w_in, i, out_dtype=BF16, tm=512, tn=512)
            y_c = spatial_gating(proj, row3(sgu_norm_g), row3(sgu_norm_b), sgu_w,
                                 jnp.transpose(sgu_b, (0, 2, 1)), i)
            y_d = sliding_window_attention(proj, swa_sinks, i)
            h = matmul([y_c, y_d], odd_w_out, i, res=h, out_dtype=F32, tm=512, tn=512)
        hq = rmsnorm(h, row3(xa_norm), layer, BF16)
        mem_n = rmsnorm(mem2, row3(mem_norm), layer, BF16)
        q = matmul([hq], xa_wq, layer, out_dtype=BF16, tm=512, tn=512)
        kv = matmul([mem_n], xa_wkv, layer, out_dtype=BF16, tm=mem2.shape[0], tn=512)
        o = cross_attention(q, kv)
        h = matmul([o], xa_wo, layer, res=h, out_dtype=F32, tm=512, tn=512)
        last = layer == DEPTH - 1
        moe_args = (h, row3(moe_norm), router_group_w, router_group_b, router_expert_w,
                    router_expert_b, w_gate, w_up, w_down, layer)
        if last:
            (out,) = hierarchical_moe(*moe_args, final_norm.reshape(1, 1, D), 0, F32, False)
        else:
            h, hn = hierarchical_moe(*moe_args, row3(mix_norm), layer + 1, BF16, True)
    return out.reshape(B, S, D)
```

```python
import functools
import math

import jax
import jax.numpy as jnp
from jax import lax
from jax.experimental import pallas as pl
from jax.experimental.pallas import tpu as pltpu

F32 = jnp.float32
BF16 = jnp.bfloat16

D_MODEL = 4096
DEPTH = 2
EPS = 1e-5
BLOCK = 128

CONV_CH = 2048
CONV_WIDTH = 31
CONV_GROUPS = 16
CONV_HALO = 32
CONV_SUB = 128

DIFF_HEADS = 8
DIFF_HD = 128
DIFF_QK = DIFF_HEADS * 2 * DIFF_HD
DIFF_V = DIFF_HEADS * 2 * DIFF_HD

SGU_CH = 2048
SGU_GROUPS = 8
SGU_CHUNK = 128

SWA_HEADS = 32
SWA_KV = 4
SWA_HD = 64
WINDOW = 128

XA_HEADS = 4
XA_HD = 256

N_GROUPS = 8
EXP_PER_GROUP = 8
N_EXPERTS = N_GROUPS * EXP_PER_GROUP
TOP_K = 2
D_FF = 384
MOE_BLOCK = 128

LANES = 128
SUBLANES = 8
LOG2E = math.log2(math.e)
VMEM_LIMIT = 56 * 1024 * 1024


def _params(sem, vmem=None):
    return pltpu.CompilerParams(dimension_semantics=sem, vmem_limit_bytes=vmem)


def _rmsnorm_kernel(x_ref, g_ref, o_ref):
    x = x_ref[...].astype(F32)
    ms = jnp.mean(x * x, axis=-1, keepdims=True)
    o_ref[...] = (x * lax.rsqrt(ms + EPS) * g_ref[...]).astype(o_ref.dtype)


def rmsnorm(x, g2, layer, out_dtype, tm=256):
    M, D = x.shape
    return pl.pallas_call(
        _rmsnorm_kernel,
        grid=(M // tm,),
        in_specs=[pl.BlockSpec((tm, D), lambda i: (i, 0)),
                  pl.BlockSpec((None, 1, D), lambda i: (layer, 0, 0))],
        out_specs=pl.BlockSpec((tm, D), lambda i: (i, 0)),
        out_shape=jax.ShapeDtypeStruct((M, D), out_dtype),
        compiler_params=_params(("arbitrary",)),
        name="rmsnorm",
    )(x, g2)


def _mm_kernel(*refs, k_splits, has_res):
    n_a = len(k_splits)
    a_refs = refs[:n_a]
    w_ref = refs[n_a]
    r_ref = refs[n_a + 1] if has_res else None
    o_ref = refs[n_a + 1 + has_res]
    wb_ref = refs[n_a + 2 + has_res]

    @pl.when(pl.program_id(1) == 0)
    def _():
        wb_ref[...] = w_ref[...].astype(BF16)

    acc = None
    off = 0
    for a_ref, ks in zip(a_refs, k_splits):
        part = jnp.dot(a_ref[...], wb_ref[off:off + ks, :], preferred_element_type=F32)
        acc = part if acc is None else acc + part
        off += ks
    if has_res:
        acc = acc + r_ref[...]
    o_ref[...] = acc.astype(o_ref.dtype)


MM_WEIGHT_BLOCK_BYTES = 8 * 1024 * 1024
MM_VMEM_BUDGET = 46 * 1024 * 1024


def _mm_tiles(M, K, N, out_bytes, has_res):
    tn = min(N, max(LANES, MM_WEIGHT_BLOCK_BYTES // (K * 4)))
    assert N % tn == 0
    for tm in (1024, 512, 256, 128):
        need = (2 * tm * K * 2 + 2 * K * tn * 4 + K * tn * 2
                + 2 * tm * tn * out_bytes + (2 * tm * tn * 4 if has_res else 0))
        if tm <= M and M % tm == 0 and need <= MM_VMEM_BUDGET:
            return tm, tn
    raise ValueError("no matmul tiling fits")


def matmul(a_list, w3, layer, *, res=None, out_dtype):
    M = a_list[0].shape[0]
    _, K, N = w3.shape
    k_splits = tuple(a.shape[1] for a in a_list)
    assert sum(k_splits) == K
    tm, tn = _mm_tiles(M, K, N, jnp.dtype(out_dtype).itemsize, res is not None)
    in_specs = [pl.BlockSpec((tm, ks), lambda j, i: (i, 0)) for ks in k_splits]
    in_specs.append(pl.BlockSpec((None, K, tn), lambda j, i: (layer, 0, j)))
    args = list(a_list) + [w3]
    if res is not None:
        in_specs.append(pl.BlockSpec((tm, tn), lambda j, i: (i, j)))
        args.append(res)
    return pl.pallas_call(
        functools.partial(_mm_kernel, k_splits=k_splits, has_res=res is not None),
        grid=(N // tn, M // tm),
        in_specs=in_specs,
        out_specs=pl.BlockSpec((tm, tn), lambda j, i: (i, j)),
        out_shape=jax.ShapeDtypeStruct((M, N), out_dtype),
        scratch_shapes=[pltpu.VMEM((K, tn), BF16)],
        compiler_params=_params(("arbitrary", "arbitrary"), VMEM_LIMIT),
        name="matmul",
    )(*args)


def _conv_kernel(a_ref, g_ref, w_ref, b_ref, ng_ref, nb_ref, o_ref, ybuf, shbuf, *, T, LB):
    i = pl.program_id(1)

    @pl.when(i == 0)
    def _():
        ybuf[0:CONV_HALO, :] = jnp.zeros((CONV_HALO, LB), F32)

    @pl.when(i > 0)
    def _():
        ybuf[0:CONV_HALO, :] = ybuf[T:T + CONV_HALO, :]

    a = a_ref[...].astype(F32)
    g = g_ref[...].astype(F32)
    ybuf[CONV_HALO:CONV_HALO + T, :] = a * jax.nn.sigmoid(g)

    first = CONV_HALO - (CONV_WIDTH - 1)
    n_sub = T // CONV_SUB

    def sub_tile(idx):
        gi, r0 = idx // n_sub, (idx % n_sub) * CONV_SUB
        cols = slice(gi * LANES, (gi + 1) * LANES)
        acc = jnp.broadcast_to(b_ref[:, cols], (CONV_SUB, LANES))
        for ph in range(SUBLANES):
            taps = range(ph, CONV_WIDTH, SUBLANES)
            n = CONV_SUB + SUBLANES * (len(taps) - 1)
            sh = shbuf.at[(idx * SUBLANES + ph) % 2]
            sh[0:n, :] = ybuf[r0 + first + ph:r0 + first + ph + n, cols]
            for a, j in enumerate(taps):
                acc = acc + w_ref[j:j + 1, cols] * sh[SUBLANES * a:SUBLANES * a + CONV_SUB, :]
        mu = jnp.mean(acc, axis=-1, keepdims=True)
        d = acc - mu
        var = jnp.mean(d * d, axis=-1, keepdims=True)
        y = d * lax.rsqrt(var + EPS) * ng_ref[:, cols] + nb_ref[:, cols]
        o_ref[r0:r0 + CONV_SUB, cols] = (y * jax.nn.sigmoid(y)).astype(o_ref.dtype)

    for idx in range((LB // LANES) * n_sub):
        sub_tile(idx)


def conformer_conv(proj, conv_w, conv_b, norm_g, norm_b, li, T=256, LB=512):
    S = proj.shape[0]
    assert CONV_CH // CONV_GROUPS == LANES
    ncb = CONV_CH // LB
    vec = lambda: pl.BlockSpec((None, 1, LB), lambda c, i: (li, 0, c))
    return pl.pallas_call(
        functools.partial(_conv_kernel, T=T, LB=LB),
        grid=(ncb, S // T),
        in_specs=[pl.BlockSpec((T, LB), lambda c, i: (i, c)),
                  pl.BlockSpec((T, LB), lambda c, i: (i, ncb + c)),
                  pl.BlockSpec((None, CONV_WIDTH, LB), lambda c, i: (li, 0, c)),
                  vec(), vec(), vec()],
        out_specs=pl.BlockSpec((T, LB), lambda c, i: (i, c)),
        out_shape=jax.ShapeDtypeStruct((S, CONV_CH), BF16),
        scratch_shapes=[pltpu.VMEM((CONV_HALO + T, LB), F32),
                        pltpu.VMEM((2, CONV_HALO + CONV_SUB, LANES), F32)],
        compiler_params=_params(("arbitrary", "arbitrary")),
        name="conformer_conv",
    )(proj, proj, conv_w, conv_b, norm_g, norm_b)


def _diffattn_kernel(q_ref, k_ref, v_ref, lam_ref, g_ref, o_ref, acc_ref, m_ref, l_ref, s0_ref, s1_ref,
                     *, tq, lambda_init):
    qi = pl.program_id(1)
    d = DIFF_HD
    tk = 2 * tq
    c_exp = (d ** -0.5) * LOG2E
    nfull = qi // 2
    acc_ref[...] = jnp.zeros_like(acc_ref)
    m_ref[...] = jnp.full_like(m_ref, -jnp.inf)
    l_ref[...] = jnp.zeros_like(l_ref)

    def qk(t, s_ref):
        kblk = k_ref[pl.ds(pl.multiple_of(t * tk, tk), tk), :]
        for c in range(2):
            s_ref[c] = lax.dot_general(q_ref[:, c * d:(c + 1) * d], kblk[:, c * d:(c + 1) * d],
                                       (((1,), (1,)), ((), ())), preferred_element_type=F32)

    def softmax_pv(t, s_ref, width, masked, row0):
        vblk = v_ref[pl.ds(pl.multiple_of(t * tk, tk), width), :]
        for c in range(2):
            s = s_ref[c, :, 0:width]
            if masked:
                row = lax.broadcasted_iota(jnp.int32, (tq, width), 0) + row0
                col = lax.broadcasted_iota(jnp.int32, (tq, width), 1)
                s = jnp.where(row >= col, s, -jnp.inf)
            m = m_ref[c]
            m_new = jnp.maximum(m, jnp.max(s, axis=-1, keepdims=True))
            alpha = jnp.exp2((m - m_new) * c_exp)
            p = jnp.exp2((s - m_new) * c_exp)
            l_ref[c] = alpha * l_ref[c] + jnp.sum(p, axis=-1, keepdims=True)
            m_ref[c] = m_new
            acc_ref[c] = alpha * acc_ref[c] + jnp.dot(p.astype(BF16), vblk,
                                                      preferred_element_type=F32)

    @pl.when(nfull % 2 == 0)
    def _():
        qk(0, s0_ref)

    @pl.when(nfull % 2 == 1)
    def _():
        qk(0, s1_ref)

    def body(t, carry):
        par = (nfull - t) % 2

        @pl.when(par == 0)
        def _():
            qk(t + 1, s1_ref)
            softmax_pv(t, s0_ref, tk, False, 0)

        @pl.when(par == 1)
        def _():
            qk(t + 1, s0_ref)
            softmax_pv(t, s1_ref, tk, False, 0)
        return carry

    lax.fori_loop(0, nfull, body, 0)

    @pl.when(qi % 2 == 0)
    def _():
        softmax_pv(nfull, s0_ref, tq, True, 0)

    @pl.when(qi % 2 == 1)
    def _():
        softmax_pv(nfull, s0_ref, tk, True, tq)

    lv = lam_ref[...]
    lam = (jnp.exp(jnp.sum(lv[0:1] * lv[1:2], axis=-1, keepdims=True))
           - jnp.exp(jnp.sum(lv[2:3] * lv[3:4], axis=-1, keepdims=True)) + lambda_init)
    o = acc_ref[0] / l_ref[0] - lam * (acc_ref[1] / l_ref[1])
    ms = jnp.mean(o * o, axis=-1, keepdims=True)
    o = o * lax.rsqrt(ms + EPS) * g_ref[...] * (1.0 - lambda_init)
    o_ref[...] = o.astype(o_ref.dtype)


def diff_attention(proj, lam_vecs, subln_g3, li, lambda_init, tq=512):
    S = proj.shape[0]
    assert S % (2 * tq) == 0
    hw = 2 * DIFF_HD
    q0 = 2 * CONV_CH // hw
    k0 = q0 + DIFF_QK // hw
    v0 = k0 + DIFF_QK // hw
    return pl.pallas_call(
        functools.partial(_diffattn_kernel, tq=tq, lambda_init=lambda_init),
        grid=(DIFF_HEADS, S // tq),
        in_specs=[pl.BlockSpec((tq, hw), lambda h, i: (i, q0 + h)),
                  pl.BlockSpec((S, hw), lambda h, i: (0, k0 + h)),
                  pl.BlockSpec((S, hw), lambda h, i: (0, v0 + h)),
                  pl.BlockSpec((None, 4, DIFF_HD), lambda h, i: (li, 0, 0)),
                  pl.BlockSpec((None, 1, hw), lambda h, i: (li, 0, 0))],
        out_specs=pl.BlockSpec((tq, hw), lambda h, i: (i, h)),
        out_shape=jax.ShapeDtypeStruct((S, DIFF_V), BF16),
        scratch_shapes=[pltpu.VMEM((2, tq, hw), F32),
                        pltpu.VMEM((2, tq, 1), F32),
                        pltpu.VMEM((2, tq, 1), F32),
                        pltpu.VMEM((2, tq, 2 * tq), F32),
                        pltpu.VMEM((2, tq, 2 * tq), F32)],
        compiler_params=_params(("arbitrary", "arbitrary"), VMEM_LIMIT),
        name="diff_attention",
    )(proj, proj, proj, lam_vecs, subln_g3)


def _sgu_kernel(u_ref, v_ref, ng_ref, nb_ref, w_ref, bs_ref, o_ref):
    T = SGU_CHUNK
    gw = SGU_CH // SGU_GROUPS
    row = lax.broadcasted_iota(jnp.int32, (T, T), 0)
    col = lax.broadcasted_iota(jnp.int32, (T, T), 1)
    causal = row >= col
    for g in range(SGU_GROUPS):
        cols = slice(g * gw, (g + 1) * gw)
        u = jax.nn.gelu(u_ref[:, cols].astype(F32), approximate=True)
        v = jax.nn.gelu(v_ref[:, cols].astype(F32), approximate=True)
        mu = jnp.mean(v, axis=-1, keepdims=True)
        d = v - mu
        var = jnp.mean(d * d, axis=-1, keepdims=True)
        vn = d * lax.rsqrt(var + EPS) * ng_ref[:, cols] + nb_ref[:, cols]
        w = jnp.where(causal, w_ref[g], 0.0).astype(BF16)
        mixed = jnp.dot(w, vn.astype(BF16), preferred_element_type=F32) + bs_ref[:, g:g + 1]
        o_ref[:, cols] = (u * mixed).astype(o_ref.dtype)


def spatial_gating(proj, norm_g3, norm_b3, w_s, b_s_t, li):
    S = proj.shape[0]
    T = SGU_CHUNK
    vec = lambda: pl.BlockSpec((None, 1, SGU_CH), lambda i: (li, 0, 0))
    return pl.pallas_call(
        _sgu_kernel,
        grid=(S // T,),
        in_specs=[pl.BlockSpec((T, SGU_CH), lambda i: (i, 0)),
                  pl.BlockSpec((T, SGU_CH), lambda i: (i, 1)),
                  vec(), vec(),
                  pl.BlockSpec((None, SGU_GROUPS, T, T), lambda i: (li, 0, 0, 0)),
                  pl.BlockSpec((None, T, SGU_GROUPS), lambda i: (li, 0, 0))],
        out_specs=pl.BlockSpec((T, SGU_CH), lambda i: (i, 0)),
        out_shape=jax.ShapeDtypeStruct((S, SGU_CH), BF16),
        compiler_params=_params(("arbitrary",)),
        name="spatial_gating",
    )(proj, proj, norm_g3, norm_b3, w_s, b_s_t)


def _swa_kernel(sink_ref, q_ref, kp_ref, kc_ref, vp_ref, vc_ref, o_ref, *, li):
    n = pl.program_id(0)
    T = BLOCK
    G = SWA_HEADS // SWA_KV
    d = SWA_HD
    kk = jnp.concatenate([kp_ref[...], kc_ref[...]], axis=0)
    vv = jnp.concatenate([vp_ref[...], vc_ref[...]], axis=0)
    qi = lax.broadcasted_iota(jnp.int32, (T, 2 * T), 0)
    kj = lax.broadcasted_iota(jnp.int32, (T, 2 * T), 1)
    rel = qi + T - kj
    valid = (rel >= 0) & (rel < WINDOW) & ((n * T + kj - T) >= 0)
    scale = d ** -0.5
    for h in range(SWA_KV):
        kh = kk[:, h * d:(h + 1) * d]
        vh = vv[:, h * d:(h + 1) * d]
        outs = []
        for g in range(G):
            c0 = (h * G + g) * d
            s = lax.dot_general(q_ref[:, c0:c0 + d], kh, (((1,), (1,)), ((), ())),
                                preferred_element_type=F32) * scale
            s = jnp.where(valid, s, -jnp.inf)
            sink = sink_ref[li, h * G + g]
            m = jnp.maximum(jnp.max(s, axis=-1, keepdims=True), sink)
            p = jnp.exp(s - m)
            denom = jnp.sum(p, axis=-1, keepdims=True) + jnp.exp(sink - m)
            outs.append(jnp.dot(p.astype(BF16), vh, preferred_element_type=F32) / denom)
        o_ref[:, h * G * d:(h + 1) * G * d] = jnp.concatenate(outs, axis=-1).astype(o_ref.dtype)


def sliding_window_attention(proj, sinks, li):
    S = proj.shape[0]
    T = BLOCK
    qw = SWA_HEADS * SWA_HD
    kw = SWA_KV * SWA_HD
    qb = 2 * SGU_CH // qw
    kb = (2 * SGU_CH + qw) // kw
    vb = kb + 1
    prev = lambda i: jnp.maximum(i - 1, 0)
    return pl.pallas_call(
        functools.partial(_swa_kernel, li=li),
        grid=(S // T,),
        in_specs=[pl.BlockSpec(memory_space=pltpu.SMEM),
                  pl.BlockSpec((T, qw), lambda i: (i, qb)),
                  pl.BlockSpec((T, kw), lambda i: (prev(i), kb)),
                  pl.BlockSpec((T, kw), lambda i: (i, kb)),
                  pl.BlockSpec((T, kw), lambda i: (prev(i), vb)),
                  pl.BlockSpec((T, kw), lambda i: (i, vb))],
        out_specs=pl.BlockSpec((T, qw), lambda i: (i, 0)),
        out_shape=jax.ShapeDtypeStruct((S, qw), BF16),
        compiler_params=_params(("arbitrary",)),
        name="sliding_window_attention",
    )(sinks, proj, proj, proj, proj, proj)


def _xattn_kernel(q_ref, kv_ref, o_ref):
    d = XA_HD
    c_exp = (d ** -0.5) * LOG2E
    for h in range(XA_HEADS):
        q = q_ref[:, h * d:(h + 1) * d]
        k = kv_ref[:, h * d:(h + 1) * d]
        v = kv_ref[:, (XA_HEADS + h) * d:(XA_HEADS + h + 1) * d]
        s = lax.dot_general(q, k, (((1,), (1,)), ((), ())), preferred_element_type=F32)
        m = jnp.max(s, axis=-1, keepdims=True)
        p = jnp.exp2((s - m) * c_exp)
        l = jnp.sum(p, axis=-1, keepdims=True)
        o = jnp.dot(p.astype(BF16), v, preferred_element_type=F32) / l
        o_ref[:, h * d:(h + 1) * d] = o.astype(o_ref.dtype)


def cross_attention(q, kv, tq=512):
    S, qw = q.shape
    M = kv.shape[0]
    return pl.pallas_call(
        _xattn_kernel,
        grid=(S // tq,),
        in_specs=[pl.BlockSpec((tq, qw), lambda i: (i, 0)),
                  pl.BlockSpec((M, 2 * qw), lambda i: (0, 0))],
        out_specs=pl.BlockSpec((tq, qw), lambda i: (i, 0)),
        out_shape=jax.ShapeDtypeStruct((S, qw), BF16),
        compiler_params=_params(("arbitrary",)),
        name="cross_attention",
    )(q, kv)


def _pack_bf16_pair(y):
    c = y.shape[1] // 2
    hi = lax.bitcast_convert_type(y[:, :c].astype(BF16).astype(F32), jnp.uint32)
    lo = lax.bitcast_convert_type(y[:, c:].astype(BF16).astype(F32), jnp.uint32)
    return hi | lax.shift_right_logical(lo, jnp.uint32(16))


def _unpack_bf16_pair(u):
    hi = lax.bitcast_convert_type(u & jnp.uint32(0xFFFF0000), F32)
    lo = lax.bitcast_convert_type(lax.shift_left(u, jnp.uint32(16)), F32)
    return hi, lo


def _router_kernel(h_ref, g_ref, wr_ref, br_ref, hn_ref, route_ref):
    x = h_ref[...]
    ms = jnp.mean(x * x, axis=-1, keepdims=True)
    hn = x * lax.rsqrt(ms + EPS) * g_ref[...]
    hn_ref[...] = _pack_bf16_pair(hn)
    hn_hi = hn.astype(BF16)
    hn_lo = (hn - hn_hi.astype(F32)).astype(BF16)
    z = jnp.dot(hn_hi, wr_ref[...], preferred_element_type=F32)
    logits = (z[:, :LANES] + z[:, LANES:]
              + jnp.dot(hn_lo, wr_ref[:, :LANES], preferred_element_type=F32) + br_ref[...])
    tm = x.shape[0]
    lane = lax.broadcasted_iota(jnp.int32, (tm, LANES), 1)
    big = jnp.int32(2 * LANES)
    gl = jnp.where(lane < N_GROUPS, logits, -jnp.inf)
    gmax = jnp.max(gl, axis=-1, keepdims=True)
    gsel = jnp.min(jnp.where(gl == gmax, lane, big), axis=-1, keepdims=True)
    g_w = 1.0 / jnp.sum(jnp.exp(gl - gmax), axis=-1, keepdims=True)
    lo = N_GROUPS + gsel * EXP_PER_GROUP
    el = jnp.where((lane >= lo) & (lane < lo + EXP_PER_GROUP), logits, -jnp.inf)
    e1 = jnp.max(el, axis=-1, keepdims=True)
    i1 = jnp.min(jnp.where(el == e1, lane, big), axis=-1, keepdims=True)
    el2 = jnp.where(lane == i1, -jnp.inf, el)
    e2 = jnp.max(el2, axis=-1, keepdims=True)
    i2 = jnp.min(jnp.where(el2 == e2, lane, big), axis=-1, keepdims=True)
    t = jnp.exp(e2 - e1)
    w1 = g_w / (1.0 + t)
    w2 = g_w * t / (1.0 + t)
    id1 = (i1 - N_GROUPS).astype(F32)
    id2 = (i2 - N_GROUPS).astype(F32)
    route = jnp.where(lane == 0, id1, jnp.where(lane == 1, id2,
                      jnp.where(lane == 2, w1, jnp.where(lane == 3, w2, 0.0))))
    route_ref[...] = route


def moe_router(h, g3, wr, br, layer, tm=256):
    S, D = h.shape
    return pl.pallas_call(
        _router_kernel,
        grid=(S // tm,),
        in_specs=[pl.BlockSpec((tm, D), lambda i: (i, 0)),
                  pl.BlockSpec((None, 1, D), lambda i: (layer, 0, 0)),
                  pl.BlockSpec((D, 2 * LANES), lambda i: (0, 0)),
                  pl.BlockSpec((1, LANES), lambda i: (0, 0))],
        out_specs=[pl.BlockSpec((tm, D // 2), lambda i: (i, 0)),
                   pl.BlockSpec((tm, LANES), lambda i: (i, 0))],
        out_shape=[jax.ShapeDtypeStruct((S, D // 2), jnp.uint32),
                   jax.ShapeDtypeStruct((S, LANES), F32)],
        compiler_params=_params(("arbitrary",), VMEM_LIMIT),
        name="moe_router",
    )(h, g3, wr, br)


def _expert_kernel(be_ref, first_ref, slot_ref, nexte_ref, nu_ref,
                   tok0_ref, tokn_ref,
                   hn_hbm, wg_hbm, wu_hbm, wd_hbm,
                   o_ref, xbuf, wgbuf, wubuf, wdbuf, xsem, wsem, *, layer):
    b = pl.program_id(0)
    nu = nu_ref[0]
    xs = b % 2

    def w_copies(e, s):
        return (pltpu.make_async_copy(wg_hbm.at[layer, e], wgbuf.at[s], wsem.at[s, 0]),
                pltpu.make_async_copy(wu_hbm.at[layer, e], wubuf.at[s], wsem.at[s, 1]),
                pltpu.make_async_copy(wd_hbm.at[layer, e], wdbuf.at[s], wsem.at[s, 2]))

    def row_copy(t, s, r):
        return pltpu.make_async_copy(hn_hbm.at[pl.ds(t, 1), :], xbuf.at[s, pl.ds(r, 1), :], xsem.at[s])

    def start_rows(tok_ref, s):
        def body(r, c):
            row_copy(tok_ref[0, r], s, r).start()
            return c
        lax.fori_loop(0, MOE_BLOCK, body, 0, unroll=8)

    def wait_rows(s):
        def body(r, c):
            row_copy(0, s, r).wait()
            return c
        lax.fori_loop(0, MOE_BLOCK, body, 0, unroll=8)

    @pl.when(b == 0)
    def _():
        start_rows(tok0_ref, 0)
        for c in w_copies(be_ref[0], 0):
            c.start(priority=1)

    @pl.when(b + 1 < nu)
    def _():
        start_rows(tokn_ref, 1 - xs)

    @pl.when(b < nu)
    def _():
        s = slot_ref[b]

        @pl.when(first_ref[b] == 1)
        def _():
            for c in w_copies(be_ref[b], s):
                c.wait()
            ne = nexte_ref[b]

            @pl.when(ne >= 0)
            def _():
                for c in w_copies(ne, 1 - s):
                    c.start(priority=1)

        wait_rows(xs)
        x_hi, x_lo = _unpack_bf16_pair(xbuf[xs])
        half = x_hi.shape[1]
        hg = (jnp.dot(x_hi, wgbuf[s, 0:half, :], preferred_element_type=F32)
              + jnp.dot(x_lo, wgbuf[s, half:, :], preferred_element_type=F32))
        hu = (jnp.dot(x_hi, wubuf[s, 0:half, :], preferred_element_type=F32)
              + jnp.dot(x_lo, wubuf[s, half:, :], preferred_element_type=F32))
        hb = (hg * jax.nn.sigmoid(hg) * hu).astype(BF16).astype(F32)
        o_ref[...] = _pack_bf16_pair(jnp.dot(hb, wdbuf[s], preferred_element_type=F32))

    @pl.when(b >= nu)
    def _():
        o_ref[...] = jnp.zeros_like(o_ref)


def moe_experts(hn_packed, plan, w_gate, w_up, w_down, layer):
    S, C = hn_packed.shape
    D = 2 * C
    nblk = plan["row_tok"].shape[0]
    anyspec = pl.BlockSpec(memory_space=pl.ANY)
    grid_spec = pltpu.PrefetchScalarGridSpec(
        num_scalar_prefetch=5,
        grid=(nblk,),
        in_specs=[pl.BlockSpec((None, 1, MOE_BLOCK), lambda b, *_: (0, 0, 0), memory_space=pltpu.SMEM),
                  pl.BlockSpec((None, 1, MOE_BLOCK), lambda b, *_: (jnp.minimum(b + 1, nblk - 1), 0, 0),
                               memory_space=pltpu.SMEM),
                  anyspec, anyspec, anyspec, anyspec],
        out_specs=pl.BlockSpec((MOE_BLOCK, C), lambda b, *_: (b, 0)),
        scratch_shapes=[pltpu.VMEM((2, MOE_BLOCK, C), jnp.uint32),
                        pltpu.VMEM((2, D, D_FF), F32),
                        pltpu.VMEM((2, D, D_FF), F32),
                        pltpu.VMEM((2, D_FF, D), F32),
                        pltpu.SemaphoreType.DMA((2,)),
                        pltpu.SemaphoreType.DMA((2, 3))],
    )
    return pl.pallas_call(
        functools.partial(_expert_kernel, layer=layer),
        grid_spec=grid_spec,
        out_shape=jax.ShapeDtypeStruct((nblk * MOE_BLOCK, C), jnp.uint32),
        compiler_params=_params(("arbitrary",), VMEM_LIMIT),
        name="moe_experts",
    )(plan["blk_e"], plan["first"], plan["slot"], plan["next_e"], plan["n_used"],
      plan["row_tok"], plan["row_tok"], hn_packed, w_gate, w_up, w_down)


def _combine_kernel(pos0_ref, posn_ref, h_ref, route_ref, g_ref, y_hbm, *rest, tm, nt, emit_h):
    if emit_h:
        h_out_ref, n_out_ref, ybuf, sem = rest
    else:
        n_out_ref, ybuf, sem = rest
    i = pl.program_id(0)
    xs = i % 2

    def row_copy(p, s, j):
        return pltpu.make_async_copy(y_hbm.at[pl.ds(p, 1), :], ybuf.at[s, pl.ds(j, 1), :], sem.at[s])

    def start_rows(pos_ref, s):
        def body(j, c):
            row_copy(pos_ref[0, 2 * j], s, 2 * j).start(priority=0)
            row_copy(pos_ref[0, 2 * j + 1], s, 2 * j + 1).start(priority=1)
            return c
        lax.fori_loop(0, TOP_K * tm // 2, body, 0, unroll=4)

    @pl.when(i == 0)
    def _():
        start_rows(pos0_ref, 0)

    @pl.when(i + 1 < nt)
    def _():
        start_rows(posn_ref, 1 - xs)

    def wait_body(j, c):
        row_copy(0, xs, j).wait()
        return c
    lax.fori_loop(0, TOP_K * tm, wait_body, 0, unroll=8)

    a_hi, a_lo = _unpack_bf16_pair(ybuf[xs, 0:tm, :])
    b_hi, b_lo = _unpack_bf16_pair(ybuf[xs, tm:2 * tm, :])
    w1 = route_ref[:, TOP_K:TOP_K + 1]
    w2 = route_ref[:, TOP_K + 1:TOP_K + 2]
    half = a_hi.shape[1]
    h_lo = h_ref[:, 0:half] + w1 * a_hi + w2 * b_hi
    h_hi = h_ref[:, half:] + w1 * a_lo + w2 * b_lo
    if emit_h:
        h_out_ref[:, 0:half] = h_lo
        h_out_ref[:, half:] = h_hi
    ms = (jnp.sum(h_lo * h_lo, axis=-1, keepdims=True)
          + jnp.sum(h_hi * h_hi, axis=-1, keepdims=True)) / (2 * half)
    r = lax.rsqrt(ms + EPS)
    n_out_ref[:, 0:half] = (h_lo * r * g_ref[:, 0:half]).astype(n_out_ref.dtype)
    n_out_ref[:, half:] = (h_hi * r * g_ref[:, half:]).astype(n_out_ref.dtype)


def moe_combine(h, route, yb, pos3, g3, g_layer, norm_dtype, emit_h, tm=128):
    S, D = h.shape
    nt = S // tm
    out_specs = [pl.BlockSpec((tm, D), lambda i: (i, 0))]
    out_shape = [jax.ShapeDtypeStruct((S, D), norm_dtype)]
    if emit_h:
        out_specs = [pl.BlockSpec((tm, D), lambda i: (i, 0))] + out_specs
        out_shape = [jax.ShapeDtypeStruct((S, D), F32)] + out_shape
    return pl.pallas_call(
        functools.partial(_combine_kernel, tm=tm, nt=nt, emit_h=emit_h),
        grid=(nt,),
        in_specs=[pl.BlockSpec((None, 1, TOP_K * tm), lambda i: (0, 0, 0), memory_space=pltpu.SMEM),
                  pl.BlockSpec((None, 1, TOP_K * tm), lambda i: (jnp.minimum(i + 1, nt - 1), 0, 0),
                               memory_space=pltpu.SMEM),
                  pl.BlockSpec((tm, D), lambda i: (i, 0)),
                  pl.BlockSpec((tm, LANES), lambda i: (i, 0)),
                  pl.BlockSpec((None, 1, D), lambda i: (g_layer, 0, 0)),
                  pl.BlockSpec(memory_space=pl.ANY)],
        out_specs=out_specs,
        out_shape=out_shape,
        scratch_shapes=[pltpu.VMEM((2, TOP_K * tm, D // 2), jnp.uint32),
                        pltpu.SemaphoreType.DMA((2,))],
        compiler_params=_params(("arbitrary",), VMEM_LIMIT),
        name="moe_combine",
    )(pos3, pos3, h, route, g3, yb)


def hierarchical_moe(h, g3, wr_g, br_g, wr_e, br_e, w_gate, w_up, w_down, layer,
                     next_g3, next_layer, next_dtype, emit_h, tm_combine=128):
    S, D = h.shape
    wr = jnp.concatenate([wr_g[layer], jnp.transpose(wr_e[layer], (1, 0, 2)).reshape(D, N_EXPERTS)],
                         axis=1)
    wr = jnp.pad(wr, ((0, 0), (0, LANES - wr.shape[1])))
    wr_hi = wr.astype(BF16)
    wr = jnp.concatenate([wr_hi, (wr - wr_hi.astype(F32)).astype(BF16)], axis=1)
    br =jnp.concatenate([br_g[layer], br_e[layer].reshape(-1)])
    br = jnp.pad(br, (0, LANES - br.shape[0])).reshape(1, LANES)
    hn_packed, route = moe_router(h, g3, wr, br, layer)
    plan = _moe_plan(route[:, 0:TOP_K].astype(jnp.int32), tm_combine)
    yb = moe_experts(hn_packed, plan, w_gate, w_up, w_down, layer)
    return moe_combine(h, route, yb, plan["pos3"], next_g3, next_layer, next_dtype, emit_h,
                       tm=tm_combine)


def _moe_plan(expert_id, tm_combine):
    S = expert_id.shape[0]
    A = S * TOP_K
    i32 = jnp.int32
    flat_e = expert_id.reshape(-1)
    order = jnp.argsort(flat_e).astype(i32)
    counts = jnp.zeros((N_EXPERTS,), i32).at[flat_e].add(1)
    start = jnp.cumsum(counts) - counts
    nb_e = (counts + MOE_BLOCK - 1) // MOE_BLOCK
    bend = jnp.cumsum(nb_e)
    pstart = (bend - nb_e) * MOE_BLOCK
    shift = pstart - start
    dshift = shift - jnp.concatenate([jnp.zeros((1,), i32), shift[:-1]])
    pos_sorted = jnp.arange(A, dtype=i32) + jnp.cumsum(
        jnp.zeros((A + 1,), i32).at[start].add(dshift))[:A]
    nblk = -(-A // MOE_BLOCK) + N_EXPERTS
    row_tok = jnp.zeros((nblk * MOE_BLOCK,), i32).at[pos_sorted].set(order // TOP_K)
    pos_flat = jnp.zeros((A,), i32).at[order].set(pos_sorted)
    n_used = bend[-1]
    ar = jnp.arange(nblk, dtype=i32)
    blk_e = jnp.minimum(jnp.searchsorted(bend, ar, side='right'), N_EXPERTS - 1).astype(i32)
    first = ((ar == 0) | (blk_e != jnp.roll(blk_e, 1))) & (ar < n_used)
    slot = (jnp.cumsum(first.astype(i32)) - 1) % 2
    big = i32(nblk)
    nf = lax.cummin(jnp.where(first, ar, big)[::-1])[::-1]
    next_first = jnp.concatenate([nf[1:], big[None]])
    next_e = jnp.where(next_first < big, blk_e[jnp.minimum(next_first, nblk - 1)], -1)
    nt = S // tm_combine
    pos3 = pos_flat.reshape(nt, tm_combine, TOP_K).transpose(0, 2, 1).reshape(nt, 1, TOP_K * tm_combine)
    return dict(row_tok=row_tok.reshape(nblk, 1, MOE_BLOCK), pos3=pos3, blk_e=blk_e,
                first=first.astype(i32), slot=slot.astype(i32), next_e=next_e.astype(i32),
                n_used=n_used.astype(i32).reshape(1))


def kernel(x, mem, mix_norm, even_w_in, conv_w, conv_b, conv_norm_g, conv_norm_b, diff_lambda, diff_subln_g, even_w_out, odd_w_in, sgu_norm_g, sgu_norm_b, sgu_w, sgu_b, swa_sinks, odd_w_out, xa_norm, mem_norm, xa_wq, xa_wkv, xa_wo, moe_norm, router_group_w, router_group_b, router_expert_w, router_expert_b, w_gate, w_up, w_down, final_norm):
    B, S, D = x.shape
    assert B == 1
    row3 = lambda p: p.reshape(p.shape[0], 1, p.shape[1])
    h = x.reshape(S, D)
    mem2 = mem.reshape(mem.shape[1], D)
    hn = rmsnorm(h, row3(mix_norm), 0, BF16)
    for layer in range(DEPTH):
        i = layer // 2
        if layer % 2 == 0:
            proj = matmul([hn], even_w_in, i, out_dtype=BF16)
            y_a = conformer_conv(proj, conv_w, row3(conv_b), row3(conv_norm_g), row3(conv_norm_b), i)
            lambda_init = 0.8 - 0.6 * math.exp(-0.3 * layer)
            y_b = diff_attention(proj, diff_lambda, row3(diff_subln_g), i, lambda_init)
            h = matmul([y_a, y_b], even_w_out, i, res=h, out_dtype=F32)
        else:
            proj = matmul([hn], odd_w_in, i, out_dtype=BF16)
            y_c = spatial_gating(proj, row3(sgu_norm_g), row3(sgu_norm_b), sgu_w,
                                 jnp.transpose(sgu_b, (0, 2, 1)), i)
            y_d = sliding_window_attention(proj, swa_sinks, i)
            h = matmul([y_c, y_d], odd_w_out, i, res=h, out_dtype=F32)
        hq = rmsnorm(h, row3(xa_norm), layer, BF16)
        mem_n = rmsnorm(mem2, row3(mem_norm), layer, BF16)
        q = matmul([hq], xa_wq, layer, out_dtype=BF16)
        kv = matmul([mem_n], xa_wkv, layer, out_dtype=BF16)
        o = cross_attention(q, kv)
        h = matmul([o], xa_wo, layer, res=h, out_dtype=F32)
        last = layer == DEPTH - 1
        moe_args = (h, row3(moe_norm), router_group_w, router_group_b, router_expert_w,
                    router_expert_b, w_gate, w_up, w_down, layer)
        if last:
            (out,) = hierarchical_moe(*moe_args, final_norm.reshape(1, 1, D), 0, F32, False)
        else:
            h, hn = hierarchical_moe(*moe_args, row3(mix_norm), layer + 1, BF16, True)
    return out.reshape(B, S, D)
```

```python
import functools
import math

import jax
import jax.numpy as jnp
from jax import lax
from jax.experimental import pallas as pl
from jax.experimental.pallas import tpu as pltpu

F32 = jnp.float32
BF16 = jnp.bfloat16

D_MODEL = 4096
DEPTH = 2
EPS = 1e-5
BLOCK = 128

CONV_CH = 2048
CONV_WIDTH = 31
CONV_GROUPS = 16
CONV_HALO = 32
CONV_SUB = 64

DIFF_HEADS = 8
DIFF_HD = 128
DIFF_QK = DIFF_HEADS * 2 * DIFF_HD
DIFF_V = DIFF_HEADS * 2 * DIFF_HD

SGU_CH = 2048
SGU_GROUPS = 8
SGU_CHUNK = 128

SWA_HEADS = 32
SWA_KV = 4
SWA_HD = 64
WINDOW = 128

XA_HEADS = 4
XA_HD = 256

N_GROUPS = 8
EXP_PER_GROUP = 8
N_EXPERTS = N_GROUPS * EXP_PER_GROUP
TOP_K = 2
D_FF = 384
MOE_BLOCK = 128

LANES = 128
SUBLANES = 8
LOG2E = math.log2(math.e)
VMEM_LIMIT = 56 * 1024 * 1024


def _params(sem, vmem=None):
    return pltpu.CompilerParams(dimension_semantics=sem, vmem_limit_bytes=vmem)


def _rmsnorm_kernel(x_ref, g_ref, o_ref):
    x = x_ref[...].astype(F32)
    ms = jnp.mean(x * x, axis=-1, keepdims=True)
    o_ref[...] = (x * lax.rsqrt(ms + EPS) * g_ref[...]).astype(o_ref.dtype)


def rmsnorm(x, g2, layer, out_dtype, tm=256):
    M, D = x.shape
    return pl.pallas_call(
        _rmsnorm_kernel,
        grid=(M // tm,),
        in_specs=[pl.BlockSpec((tm, D), lambda i: (i, 0)),
                  pl.BlockSpec((None, 1, D), lambda i: (layer, 0, 0))],
        out_specs=pl.BlockSpec((tm, D), lambda i: (i, 0)),
        out_shape=jax.ShapeDtypeStruct((M, D), out_dtype),
        compiler_params=_params(("arbitrary",)),
        name="rmsnorm",
    )(x, g2)


def _mm_kernel(*refs, k_splits, has_res):
    n_a = len(k_splits)
    a_refs = refs[:n_a]
    w_ref = refs[n_a]
    r_ref = refs[n_a + 1] if has_res else None
    o_ref = refs[n_a + 1 + has_res]
    wb_ref = refs[n_a + 2 + has_res]

    @pl.when(pl.program_id(1) == 0)
    def _():
        wb_ref[...] = w_ref[...].astype(BF16)

    acc = None
    off = 0
    for a_ref, ks in zip(a_refs, k_splits):
        part = jnp.dot(a_ref[...], wb_ref[off:off + ks, :], preferred_element_type=F32)
        acc = part if acc is None else acc + part
        off += ks
    if has_res:
        acc = acc + r_ref[...]
    o_ref[...] = acc.astype(o_ref.dtype)


MM_WEIGHT_BLOCK_BYTES = 8 * 1024 * 1024
MM_VMEM_BUDGET = 46 * 1024 * 1024


def _mm_tiles(M, K, N, out_bytes, has_res):
    tn = min(N, max(LANES, MM_WEIGHT_BLOCK_BYTES // (K * 4)))
    assert N % tn == 0
    for tm in (1024, 512, 256, 128):
        need = (2 * tm * K * 2 + 2 * K * tn * 4 + K * tn * 2
                + 2 * tm * tn * out_bytes + (2 * tm * tn * 4 if has_res else 0))
        if tm <= M and M % tm == 0 and need <= MM_VMEM_BUDGET:
            return tm, tn
    raise ValueError("no matmul tiling fits")


def matmul(a_list, w3, layer, *, res=None, out_dtype):
    M = a_list[0].shape[0]
    _, K, N = w3.shape
    k_splits = tuple(a.shape[1] for a in a_list)
    assert sum(k_splits) == K
    tm, tn = _mm_tiles(M, K, N, jnp.dtype(out_dtype).itemsize, res is not None)
    in_specs = [pl.BlockSpec((tm, ks), lambda j, i: (i, 0)) for ks in k_splits]
    in_specs.append(pl.BlockSpec((None, K, tn), lambda j, i: (layer, 0, j)))
    args = list(a_list) + [w3]
    if res is not None:
        in_specs.append(pl.BlockSpec((tm, tn), lambda j, i: (i, j)))
        args.append(res)
    return pl.pallas_call(
        functools.partial(_mm_kernel, k_splits=k_splits, has_res=res is not None),
        grid=(N // tn, M // tm),
        in_specs=in_specs,
        out_specs=pl.BlockSpec((tm, tn), lambda j, i: (i, j)),
        out_shape=jax.ShapeDtypeStruct((M, N), out_dtype),
        scratch_shapes=[pltpu.VMEM((K, tn), BF16)],
        compiler_params=_params(("arbitrary", "arbitrary"), VMEM_LIMIT),
        name="matmul",
    )(*args)


def _conv_kernel(a_ref, g_ref, w_ref, b_ref, ng_ref, nb_ref, o_ref, ybuf, stage, shbuf, *, T, LB):
    i = pl.program_id(1)

    @pl.when(i == 0)
    def _():
        ybuf[0:CONV_HALO, :] = jnp.zeros((CONV_HALO, LB), F32)

    @pl.when(i > 0)
    def _():
        ybuf[0:CONV_HALO, :] = ybuf[T:T + CONV_HALO, :]

    a = a_ref[...].astype(F32)
    g = g_ref[...].astype(F32)
    ybuf[CONV_HALO:CONV_HALO + T, :] = a * jax.nn.sigmoid(g)

    first = CONV_HALO - (CONV_WIDTH - 1)
    n_sub = T // CONV_SUB

    def sub_tile(it, carry):
        r0 = pl.multiple_of(it * CONV_SUB, CONV_SUB)
        stage[...] = ybuf[pl.ds(r0, CONV_SUB + CONV_HALO), :]
        for gi in range(LB // LANES):
            cols = slice(gi * LANES, (gi + 1) * LANES)
            acc = jnp.broadcast_to(b_ref[:, cols], (CONV_SUB, LANES))
            for ph in range(SUBLANES):
                taps = range(ph, CONV_WIDTH, SUBLANES)
                n = CONV_SUB + SUBLANES * (len(taps) - 1)
                sh = shbuf.at[gi, ph % 2]
                sh[0:n, :] = stage[first + ph:first + ph + n, cols]
                for a, j in enumerate(taps):
                    acc = acc + w_ref[j:j + 1, cols] * sh[SUBLANES * a:SUBLANES * a + CONV_SUB, :]
            mu = jnp.mean(acc, axis=-1, keepdims=True)
            d = acc - mu
            var = jnp.mean(d * d, axis=-1, keepdims=True)
            y = d * lax.rsqrt(var + EPS) * ng_ref[:, cols] + nb_ref[:, cols]
            o_ref[pl.ds(r0, CONV_SUB), cols] = (y * jax.nn.sigmoid(y)).astype(o_ref.dtype)
        return carry

    lax.fori_loop(0, n_sub, sub_tile, 0)


def conformer_conv(proj, conv_w, conv_b, norm_g, norm_b, li, T=256, LB=512):
    S = proj.shape[0]
    assert CONV_CH // CONV_GROUPS == LANES
    ncb = CONV_CH // LB
    vec = lambda: pl.BlockSpec((None, 1, LB), lambda c, i: (li, 0, c))
    return pl.pallas_call(
        functools.partial(_conv_kernel, T=T, LB=LB),
        grid=(ncb, S // T),
        in_specs=[pl.BlockSpec((T, LB), lambda c, i: (i, c)),
                  pl.BlockSpec((T, LB), lambda c, i: (i, ncb + c)),
                  pl.BlockSpec((None, CONV_WIDTH, LB), lambda c, i: (li, 0, c)),
                  vec(), vec(), vec()],
        out_specs=pl.BlockSpec((T, LB), lambda c, i: (i, c)),
        out_shape=jax.ShapeDtypeStruct((S, CONV_CH), BF16),
        scratch_shapes=[pltpu.VMEM((CONV_HALO + T, LB), F32),
                        pltpu.VMEM((CONV_HALO + CONV_SUB, LB), F32),
                        pltpu.VMEM((LB // LANES, 2, CONV_HALO + CONV_SUB, LANES), F32)],
        compiler_params=_params(("arbitrary", "arbitrary")),
        name="conformer_conv",
    )(proj, proj, conv_w, conv_b, norm_g, norm_b)


def _diffattn_kernel(q_ref, k_ref, v_ref, lam_ref, g_ref, o_ref, acc_ref, m_ref, l_ref, s0_ref, s1_ref,
                     *, tq, lambda_init):
    qi = pl.program_id(1)
    d = DIFF_HD
    tk = 2 * tq
    c_exp = (d ** -0.5) * LOG2E
    nfull = qi // 2
    acc_ref[...] = jnp.zeros_like(acc_ref)
    m_ref[...] = jnp.full_like(m_ref, -jnp.inf)
    l_ref[...] = jnp.zeros_like(l_ref)

    def qk(t, s_ref):
        kblk = k_ref[pl.ds(pl.multiple_of(t * tk, tk), tk), :]
        for c in range(2):
            s_ref[c] = lax.dot_general(q_ref[:, c * d:(c + 1) * d], kblk[:, c * d:(c + 1) * d],
                                       (((1,), (1,)), ((), ())), preferred_element_type=F32)

    def softmax_pv(t, s_ref, width, masked, row0):
        vblk = v_ref[pl.ds(pl.multiple_of(t * tk, tk), width), :]
        for c in range(2):
            s = s_ref[c, :, 0:width]
            if masked:
                row = lax.broadcasted_iota(jnp.int32, (tq, width), 0) + row0
                col = lax.broadcasted_iota(jnp.int32, (tq, width), 1)
                s = jnp.where(row >= col, s, -jnp.inf)
            m = m_ref[c]
            m_new = jnp.maximum(m, jnp.max(s, axis=-1, keepdims=True))
            alpha = jnp.exp2((m - m_new) * c_exp)
            p = jnp.exp2((s - m_new) * c_exp)
            l_ref[c] = alpha * l_ref[c] + jnp.sum(p, axis=-1, keepdims=True)
            m_ref[c] = m_new
            acc_ref[c] = alpha * acc_ref[c] + jnp.dot(p.astype(BF16), vblk,
                                                      preferred_element_type=F32)

    @pl.when(nfull % 2 == 0)
    def _():
        qk(0, s0_ref)

    @pl.when(nfull % 2 == 1)
    def _():
        qk(0, s1_ref)

    def body(t, carry):
        par = (nfull - t) % 2

        @pl.when(par == 0)
        def _():
            qk(t + 1, s1_ref)
            softmax_pv(t, s0_ref, tk, False, 0)

        @pl.when(par == 1)
        def _():
            qk(t + 1, s0_ref)
            softmax_pv(t, s1_ref, tk, False, 0)
        return carry

    lax.fori_loop(0, nfull, body, 0)

    @pl.when(qi % 2 == 0)
    def _():
        softmax_pv(nfull, s0_ref, tq, True, 0)

    @pl.when(qi % 2 == 1)
    def _():
        softmax_pv(nfull, s0_ref, tk, True, tq)

    lv = lam_ref[...]
    lam = (jnp.exp(jnp.sum(lv[0:1] * lv[1:2], axis=-1, keepdims=True))
           - jnp.exp(jnp.sum(lv[2:3] * lv[3:4], axis=-1, keepdims=True)) + lambda_init)
    o = acc_ref[0] / l_ref[0] - lam * (acc_ref[1] / l_ref[1])
    ms = jnp.mean(o * o, axis=-1, keepdims=True)
    o = o * lax.rsqrt(ms + EPS) * g_ref[...] * (1.0 - lambda_init)
    o_ref[...] = o.astype(o_ref.dtype)


def diff_attention(proj, lam_vecs, subln_g3, li, lambda_init, tq=512):
    S = proj.shape[0]
    assert S % (2 * tq) == 0
    hw = 2 * DIFF_HD
    q0 = 2 * CONV_CH // hw
    k0 = q0 + DIFF_QK // hw
    v0 = k0 + DIFF_QK // hw
    return pl.pallas_call(
        functools.partial(_diffattn_kernel, tq=tq, lambda_init=lambda_init),
        grid=(DIFF_HEADS, S // tq),
        in_specs=[pl.BlockSpec((tq, hw), lambda h, i: (i, q0 + h)),
                  pl.BlockSpec((S, hw), lambda h, i: (0, k0 + h)),
                  pl.BlockSpec((S, hw), lambda h, i: (0, v0 + h)),
                  pl.BlockSpec((None, 4, DIFF_HD), lambda h, i: (li, 0, 0)),
                  pl.BlockSpec((None, 1, hw), lambda h, i: (li, 0, 0))],
        out_specs=pl.BlockSpec((tq, hw), lambda h, i: (i, h)),
        out_shape=jax.ShapeDtypeStruct((S, DIFF_V), BF16),
        scratch_shapes=[pltpu.VMEM((2, tq, hw), F32),
                        pltpu.VMEM((2, tq, 1), F32),
                        pltpu.VMEM((2, tq, 1), F32),
                        pltpu.VMEM((2, tq, 2 * tq), F32),
                        pltpu.VMEM((2, tq, 2 * tq), F32)],
        compiler_params=_params(("arbitrary", "arbitrary"), VMEM_LIMIT),
        name="diff_attention",
    )(proj, proj, proj, lam_vecs, subln_g3)


def _sgu_kernel(u_ref, v_ref, ng_ref, nb_ref, w_ref, bs_ref, o_ref):
    T = SGU_CHUNK
    gw = SGU_CH // SGU_GROUPS
    row = lax.broadcasted_iota(jnp.int32, (T, T), 0)
    col = lax.broadcasted_iota(jnp.int32, (T, T), 1)
    causal = row >= col
    for g in range(SGU_GROUPS):
        cols = slice(g * gw, (g + 1) * gw)
        u = jax.nn.gelu(u_ref[:, cols].astype(F32), approximate=True)
        v = jax.nn.gelu(v_ref[:, cols].astype(F32), approximate=True)
        mu = jnp.mean(v, axis=-1, keepdims=True)
        d = v - mu
        var = jnp.mean(d * d, axis=-1, keepdims=True)
        vn = d * lax.rsqrt(var + EPS) * ng_ref[:, cols] + nb_ref[:, cols]
        w = jnp.where(causal, w_ref[g], 0.0).astype(BF16)
        mixed = jnp.dot(w, vn.astype(BF16), preferred_element_type=F32) + bs_ref[:, g:g + 1]
        o_ref[:, cols] = (u * mixed).astype(o_ref.dtype)


def spatial_gating(proj, norm_g3, norm_b3, w_s, b_s_t, li):
    S = proj.shape[0]
    T = SGU_CHUNK
    vec = lambda: pl.BlockSpec((None, 1, SGU_CH), lambda i: (li, 0, 0))
    return pl.pallas_call(
        _sgu_kernel,
        grid=(S // T,),
        in_specs=[pl.BlockSpec((T, SGU_CH), lambda i: (i, 0)),
                  pl.BlockSpec((T, SGU_CH), lambda i: (i, 1)),
                  vec(), vec(),
                  pl.BlockSpec((None, SGU_GROUPS, T, T), lambda i: (li, 0, 0, 0)),
                  pl.BlockSpec((None, T, SGU_GROUPS), lambda i: (li, 0, 0))],
        out_specs=pl.BlockSpec((T, SGU_CH), lambda i: (i, 0)),
        out_shape=jax.ShapeDtypeStruct((S, SGU_CH), BF16),
        compiler_params=_params(("arbitrary",)),
        name="spatial_gating",
    )(proj, proj, norm_g3, norm_b3, w_s, b_s_t)


def _swa_kernel(sink_ref, q_ref, kp_ref, kc_ref, vp_ref, vc_ref, o_ref, *, li):
    n = pl.program_id(0)
    T = BLOCK
    G = SWA_HEADS // SWA_KV
    d = SWA_HD
    kk = jnp.concatenate([kp_ref[...], kc_ref[...]], axis=0)
    vv = jnp.concatenate([vp_ref[...], vc_ref[...]], axis=0)
    qi = lax.broadcasted_iota(jnp.int32, (T, 2 * T), 0)
    kj = lax.broadcasted_iota(jnp.int32, (T, 2 * T), 1)
    rel = qi + T - kj
    valid = (rel >= 0) & (rel < WINDOW) & ((n * T + kj - T) >= 0)
    scale = d ** -0.5
    for h in range(SWA_KV):
        kh = kk[:, h * d:(h + 1) * d]
        vh = vv[:, h * d:(h + 1) * d]
        outs = []
        for g in range(G):
            c0 = (h * G + g) * d
            s = lax.dot_general(q_ref[:, c0:c0 + d], kh, (((1,), (1,)), ((), ())),
                                preferred_element_type=F32) * scale
            s = jnp.where(valid, s, -jnp.inf)
            sink = sink_ref[li, h * G + g]
            m = jnp.maximum(jnp.max(s, axis=-1, keepdims=True), sink)
            p = jnp.exp(s - m)
            denom = jnp.sum(p, axis=-1, keepdims=True) + jnp.exp(sink - m)
            outs.append(jnp.dot(p.astype(BF16), vh, preferred_element_type=F32) / denom)
        o_ref[:, h * G * d:(h + 1) * G * d] = jnp.concatenate(outs, axis=-1).astype(o_ref.dtype)


def sliding_window_attention(proj, sinks, li):
    S = proj.shape[0]
    T = BLOCK
    qw = SWA_HEADS * SWA_HD
    kw = SWA_KV * SWA_HD
    qb = 2 * SGU_CH // qw
    kb = (2 * SGU_CH + qw) // kw
    vb = kb + 1
    prev = lambda i: jnp.maximum(i - 1, 0)
    return pl.pallas_call(
        functools.partial(_swa_kernel, li=li),
        grid=(S // T,),
        in_specs=[pl.BlockSpec(memory_space=pltpu.SMEM),
                  pl.BlockSpec((T, qw), lambda i: (i, qb)),
                  pl.BlockSpec((T, kw), lambda i: (prev(i), kb)),
                  pl.BlockSpec((T, kw), lambda i: (i, kb)),
                  pl.BlockSpec((T, kw), lambda i: (prev(i), vb)),
                  pl.BlockSpec((T, kw), lambda i: (i, vb))],
        out_specs=pl.BlockSpec((T, qw), lambda i: (i, 0)),
        out_shape=jax.ShapeDtypeStruct((S, qw), BF16),
        compiler_params=_params(("arbitrary",)),
        name="sliding_window_attention",
    )(sinks, proj, proj, proj, proj, proj)


def _xattn_kernel(q_ref, kv_ref, o_ref):
    d = XA_HD
    c_exp = (d ** -0.5) * LOG2E
    for h in range(XA_HEADS):
        q = q_ref[:, h * d:(h + 1) * d]
        k = kv_ref[:, h * d:(h + 1) * d]
        v = kv_ref[:, (XA_HEADS + h) * d:(XA_HEADS + h + 1) * d]
        s = lax.dot_general(q, k, (((1,), (1,)), ((), ())), preferred_element_type=F32)
        m = jnp.max(s, axis=-1, keepdims=True)
        p = jnp.exp2((s - m) * c_exp)
        l = jnp.sum(p, axis=-1, keepdims=True)
        o = jnp.dot(p.astype(BF16), v, preferred_element_type=F32) / l
        o_ref[:, h * d:(h + 1) * d] = o.astype(o_ref.dtype)


def cross_attention(q, kv, tq=512):
    S, qw = q.shape
    M = kv.shape[0]
    return pl.pallas_call(
        _xattn_kernel,
        grid=(S // tq,),
        in_specs=[pl.BlockSpec((tq, qw), lambda i: (i, 0)),
                  pl.BlockSpec((M, 2 * qw), lambda i: (0, 0))],
        out_specs=pl.BlockSpec((tq, qw), lambda i: (i, 0)),
        out_shape=jax.ShapeDtypeStruct((S, qw), BF16),
        compiler_params=_params(("arbitrary",)),
        name="cross_attention",
    )(q, kv)


def _pack_bf16_pair(y):
    c = y.shape[1] // 2
    hi = lax.bitcast_convert_type(y[:, :c].astype(BF16).astype(F32), jnp.uint32)
    lo = lax.bitcast_convert_type(y[:, c:].astype(BF16).astype(F32), jnp.uint32)
    return hi | lax.shift_right_logical(lo, jnp.uint32(16))


def _unpack_bf16_pair(u):
    hi = lax.bitcast_convert_type(u & jnp.uint32(0xFFFF0000), F32)
    lo = lax.bitcast_convert_type(lax.shift_left(u, jnp.uint32(16)), F32)
    return hi, lo


def _router_kernel(h_ref, g_ref, wr_ref, br_ref, hn_ref, route_ref):
    x = h_ref[...]
    ms = jnp.mean(x * x, axis=-1, keepdims=True)
    hn = x * lax.rsqrt(ms + EPS) * g_ref[...]
    hn_ref[...] = _pack_bf16_pair(hn)
    hn_hi = hn.astype(BF16)
    hn_lo = (hn - hn_hi.astype(F32)).astype(BF16)
    z = jnp.dot(hn_hi, wr_ref[...], preferred_element_type=F32)
    logits = (z[:, :LANES] + z[:, LANES:]
              + jnp.dot(hn_lo, wr_ref[:, :LANES], preferred_element_type=F32) + br_ref[...])
    tm = x.shape[0]
    lane = lax.broadcasted_iota(jnp.int32, (tm, LANES), 1)
    big = jnp.int32(2 * LANES)
    gl = jnp.where(lane < N_GROUPS, logits, -jnp.inf)
    gmax = jnp.max(gl, axis=-1, keepdims=True)
    gsel = jnp.min(jnp.where(gl == gmax, lane, big), axis=-1, keepdims=True)
    g_w = 1.0 / jnp.sum(jnp.exp(gl - gmax), axis=-1, keepdims=True)
    lo = N_GROUPS + gsel * EXP_PER_GROUP
    el = jnp.where((lane >= lo) & (lane < lo + EXP_PER_GROUP), logits, -jnp.inf)
    e1 = jnp.max(el, axis=-1, keepdims=True)
    i1 = jnp.min(jnp.where(el == e1, lane, big), axis=-1, keepdims=True)
    el2 = jnp.where(lane == i1, -jnp.inf, el)
    e2 = jnp.max(el2, axis=-1, keepdims=True)
    i2 = jnp.min(jnp.where(el2 == e2, lane, big), axis=-1, keepdims=True)
    t = jnp.exp(e2 - e1)
    w1 = g_w / (1.0 + t)
    w2 = g_w * t / (1.0 + t)
    id1 = (i1 - N_GROUPS).astype(F32)
    id2 = (i2 - N_GROUPS).astype(F32)
    route = jnp.where(lane == 0, id1, jnp.where(lane == 1, id2,
                      jnp.where(lane == 2, w1, jnp.where(lane == 3, w2, 0.0))))
    route_ref[...] = route


def moe_router(h, g3, wr, br, layer, tm=256):
    S, D = h.shape
    return pl.pallas_call(
        _router_kernel,
        grid=(S // tm,),
        in_specs=[pl.BlockSpec((tm, D), lambda i: (i, 0)),
                  pl.BlockSpec((None, 1, D), lambda i: (layer, 0, 0)),
                  pl.BlockSpec((D, 2 * LANES), lambda i: (0, 0)),
                  pl.BlockSpec((1, LANES), lambda i: (0, 0))],
        out_specs=[pl.BlockSpec((tm, D // 2), lambda i: (i, 0)),
                   pl.BlockSpec((tm, LANES), lambda i: (i, 0))],
        out_shape=[jax.ShapeDtypeStruct((S, D // 2), jnp.uint32),
                   jax.ShapeDtypeStruct((S, LANES), F32)],
        compiler_params=_params(("arbitrary",), VMEM_LIMIT),
        name="moe_router",
    )(h, g3, wr, br)


def _dispatch_kernel(nu_ref, tok0_ref, tokn_ref, hn_hbm, o_ref, buf, sem):
    b = pl.program_id(0)
    nu = nu_ref[0]
    xs = b % 2

    def row_copy(t, s, r):
        return pltpu.make_async_copy(hn_hbm.at[pl.ds(t, 1), :], buf.at[s, pl.ds(r, 1), :], sem.at[s])

    def start_rows(tok_ref, s):
        def body(j, c):
            row_copy(tok_ref[0, 2 * j], s, 2 * j).start(priority=0)
            row_copy(tok_ref[0, 2 * j + 1], s, 2 * j + 1).start(priority=1)
            return c
        lax.fori_loop(0, MOE_BLOCK // 2, body, 0, unroll=4)

    @pl.when(b == 0)
    def _():
        start_rows(tok0_ref, 0)

    @pl.when(b + 1 < nu)
    def _():
        start_rows(tokn_ref, 1 - xs)

    @pl.when(b < nu)
    def _():
        def wait_body(r, c):
            row_copy(0, xs, r).wait()
            return c
        lax.fori_loop(0, MOE_BLOCK, wait_body, 0, unroll=8)
        o_ref[...] = buf[xs]

    @pl.when(b >= nu)
    def _():
        o_ref[...] = jnp.zeros_like(o_ref)


def moe_dispatch(hn_packed, plan):
    S, C = hn_packed.shape
    nblk = plan["row_tok"].shape[0]
    grid_spec = pltpu.PrefetchScalarGridSpec(
        num_scalar_prefetch=1,
        grid=(nblk,),
        in_specs=[pl.BlockSpec((None, 1, MOE_BLOCK), lambda b, nu: (0, 0, 0), memory_space=pltpu.SMEM),
                  pl.BlockSpec((None, 1, MOE_BLOCK), lambda b, nu: (jnp.minimum(b + 1, nblk - 1), 0, 0),
                               memory_space=pltpu.SMEM),
                  pl.BlockSpec(memory_space=pl.ANY)],
        out_specs=pl.BlockSpec((MOE_BLOCK, C), lambda b, nu: (b, 0)),
        scratch_shapes=[pltpu.VMEM((2, MOE_BLOCK, C), jnp.uint32),
                        pltpu.SemaphoreType.DMA((2,))],
    )
    return pl.pallas_call(
        _dispatch_kernel,
        grid_spec=grid_spec,
        out_shape=jax.ShapeDtypeStruct((nblk * MOE_BLOCK, C), jnp.uint32),
        compiler_params=_params(("arbitrary",)),
        name="moe_dispatch",
    )(plan["n_used"], plan["row_tok"], plan["row_tok"], hn_packed)


def _expert_kernel(be_ref, first_ref, slot_ref, nexte_ref, nu_ref,
                   x_ref, wg_hbm, wu_hbm, wd_hbm,
                   o_ref, wgbuf, wubuf, wdbuf, wsem, *, layer):
    b = pl.program_id(0)
    nu = nu_ref[0]

    def w_copies(e, s):
        return (pltpu.make_async_copy(wg_hbm.at[layer, e], wgbuf.at[s], wsem.at[s, 0]),
                pltpu.make_async_copy(wu_hbm.at[layer, e], wubuf.at[s], wsem.at[s, 1]),
                pltpu.make_async_copy(wd_hbm.at[layer, e], wdbuf.at[s], wsem.at[s, 2]))

    @pl.when(b == 0)
    def _():
        for c in w_copies(be_ref[0], 0):
            c.start()

    @pl.when(b < nu)
    def _():
        s = slot_ref[b]

        @pl.when(first_ref[b] == 1)
        def _():
            for c in w_copies(be_ref[b], s):
                c.wait()
            ne = nexte_ref[b]

            @pl.when(ne >= 0)
            def _():
                for c in w_copies(ne, 1 - s):
                    c.start()

        x_hi, x_lo = _unpack_bf16_pair(x_ref[...])
        half = x_hi.shape[1]
        hg = (jnp.dot(x_hi, wgbuf[s, 0:half, :], preferred_element_type=F32)
              + jnp.dot(x_lo, wgbuf[s, half:, :], preferred_element_type=F32))
        hu = (jnp.dot(x_hi, wubuf[s, 0:half, :], preferred_element_type=F32)
              + jnp.dot(x_lo, wubuf[s, half:, :], preferred_element_type=F32))
        hb = (hg * jax.nn.sigmoid(hg) * hu).astype(BF16).astype(F32)
        o_ref[...] = _pack_bf16_pair(jnp.dot(hb, wdbuf[s], preferred_element_type=F32))

    @pl.when(b >= nu)
    def _():
        o_ref[...] = jnp.zeros_like(o_ref)


def moe_experts(xb, plan, w_gate, w_up, w_down, layer):
    R, C = xb.shape
    D = 2 * C
    nblk = R // MOE_BLOCK
    anyspec = pl.BlockSpec(memory_space=pl.ANY)
    grid_spec = pltpu.PrefetchScalarGridSpec(
        num_scalar_prefetch=5,
        grid=(nblk,),
        in_specs=[pl.BlockSpec((MOE_BLOCK, C), lambda b, *_: (b, 0)),
                  anyspec, anyspec, anyspec],
        out_specs=pl.BlockSpec((MOE_BLOCK, C), lambda b, *_: (b, 0)),
        scratch_shapes=[pltpu.VMEM((2, D, D_FF), F32),
                        pltpu.VMEM((2, D, D_FF), F32),
                        pltpu.VMEM((2, D_FF, D), F32),
                        pltpu.SemaphoreType.DMA((2, 3))],
    )
    return pl.pallas_call(
        functools.partial(_expert_kernel, layer=layer),
        grid_spec=grid_spec,
        out_shape=jax.ShapeDtypeStruct((R, C), jnp.uint32),
        compiler_params=_params(("arbitrary",), VMEM_LIMIT),
        name="moe_experts",
    )(plan["blk_e"], plan["first"], plan["slot"], plan["next_e"], plan["n_used"],
      xb, w_gate, w_up, w_down)


def _combine_kernel(pos0_ref, posn_ref, h_ref, route_ref, g_ref, y_hbm, *rest, tm, nt, emit_h):
    if emit_h:
        h_out_ref, n_out_ref, ybuf, sem = rest
    else:
        n_out_ref, ybuf, sem = rest
    i = pl.program_id(0)
    xs = i % 2

    def row_copy(p, s, j):
        return pltpu.make_async_copy(y_hbm.at[pl.ds(p, 1), :], ybuf.at[s, pl.ds(j, 1), :], sem.at[s])

    def start_rows(pos_ref, s):
        def body(j, c):
            row_copy(pos_ref[0, 2 * j], s, 2 * j).start(priority=0)
            row_copy(pos_ref[0, 2 * j + 1], s, 2 * j + 1).start(priority=1)
            return c
        lax.fori_loop(0, TOP_K * tm // 2, body, 0, unroll=4)

    @pl.when(i == 0)
    def _():
        start_rows(pos0_ref, 0)

    @pl.when(i + 1 < nt)
    def _():
        start_rows(posn_ref, 1 - xs)

    def wait_body(j, c):
        row_copy(0, xs, j).wait()
        return c
    lax.fori_loop(0, TOP_K * tm, wait_body, 0, unroll=8)

    a_hi, a_lo = _unpack_bf16_pair(ybuf[xs, 0:tm, :])
    b_hi, b_lo = _unpack_bf16_pair(ybuf[xs, tm:2 * tm, :])
    w1 = route_ref[:, TOP_K:TOP_K + 1]
    w2 = route_ref[:, TOP_K + 1:TOP_K + 2]
    half = a_hi.shape[1]
    h_lo = h_ref[:, 0:half] + w1 * a_hi + w2 * b_hi
    h_hi = h_ref[:, half:] + w1 * a_lo + w2 * b_lo
    if emit_h:
        h_out_ref[:, 0:half] = h_lo
        h_out_ref[:, half:] = h_hi
    ms = (jnp.sum(h_lo * h_lo, axis=-1, keepdims=True)
          + jnp.sum(h_hi * h_hi, axis=-1, keepdims=True)) / (2 * half)
    r = lax.rsqrt(ms + EPS)
    n_out_ref[:, 0:half] = (h_lo * r * g_ref[:, 0:half]).astype(n_out_ref.dtype)
    n_out_ref[:, half:] = (h_hi * r * g_ref[:, half:]).astype(n_out_ref.dtype)


def moe_combine(h, route, yb, pos3, g3, g_layer, norm_dtype, emit_h, tm=128):
    S, D = h.shape
    nt = S // tm
    out_specs = [pl.BlockSpec((tm, D), lambda i: (i, 0))]
    out_shape = [jax.ShapeDtypeStruct((S, D), norm_dtype)]
    if emit_h:
        out_specs = [pl.BlockSpec((tm, D), lambda i: (i, 0))] + out_specs
        out_shape = [jax.ShapeDtypeStruct((S, D), F32)] + out_shape
    return pl.pallas_call(
        functools.partial(_combine_kernel, tm=tm, nt=nt, emit_h=emit_h),
        grid=(nt,),
        in_specs=[pl.BlockSpec((None, 1, TOP_K * tm), lambda i: (0, 0, 0), memory_space=pltpu.SMEM),
                  pl.BlockSpec((None, 1, TOP_K * tm), lambda i: (jnp.minimum(i + 1, nt - 1), 0, 0),
                               memory_space=pltpu.SMEM),
                  pl.BlockSpec((tm, D), lambda i: (i, 0)),
                  pl.BlockSpec((tm, LANES), lambda i: (i, 0)),
                  pl.BlockSpec((None, 1, D), lambda i: (g_layer, 0, 0)),
                  pl.BlockSpec(memory_space=pl.ANY)],
        out_specs=out_specs,
        out_shape=out_shape,
        scratch_shapes=[pltpu.VMEM((2, TOP_K * tm, D // 2), jnp.uint32),
                        pltpu.SemaphoreType.DMA((2,))],
        compiler_params=_params(("arbitrary",), VMEM_LIMIT),
        name="moe_combine",
    )(pos3, pos3, h, route, g3, yb)


def hierarchical_moe(h, g3, wr_g, br_g, wr_e, br_e, w_gate, w_up, w_down, layer,
                     next_g3, next_layer, next_dtype, emit_h, tm_combine=128):
    S, D = h.shape
    wr = jnp.concatenate([wr_g[layer], jnp.transpose(wr_e[layer], (1, 0, 2)).reshape(D, N_EXPERTS)],
                         axis=1)
    wr = jnp.pad(wr, ((0, 0), (0, LANES - wr.shape[1])))
    wr_hi = wr.astype(BF16)
    wr = jnp.concatenate([wr_hi, (wr - wr_hi.astype(F32)).astype(BF16)], axis=1)
    br =jnp.concatenate([br_g[layer], br_e[layer].reshape(-1)])
    br = jnp.pad(br, (0, LANES - br.shape[0])).reshape(1, LANES)
    hn_packed, route = moe_router(h, g3, wr, br, layer)
    plan = _moe_plan(route[:, 0:TOP_K].astype(jnp.int32), tm_combine)
    yb = moe_experts(moe_dispatch(hn_packed, plan), plan, w_gate, w_up, w_down, layer)
    return moe_combine(h, route, yb, plan["pos3"], next_g3, next_layer, next_dtype, emit_h,
                       tm=tm_combine)


def _moe_plan(expert_id, tm_combine):
    S = expert_id.shape[0]
    A = S * TOP_K
    i32 = jnp.int32
    flat_e = expert_id.reshape(-1)
    order = jnp.argsort(flat_e).astype(i32)
    rank = jnp.argsort(order).astype(i32)
    edges = jnp.searchsorted(flat_e[order], jnp.arange(N_EXPERTS + 1, dtype=i32), side='left').astype(i32)
    start, counts = edges[:-1], edges[1:] - edges[:-1]
    nb_e = (counts + MOE_BLOCK - 1) // MOE_BLOCK
    bend = jnp.cumsum(nb_e)
    pstart = (bend - nb_e) * MOE_BLOCK
    shift = pstart - start
    dshift = shift - jnp.concatenate([jnp.zeros((1,), i32), shift[:-1]])
    pos_flat = rank + jnp.sum(jnp.where(rank[:, None] >= start[None, :], dshift[None, :], 0), axis=1)
    nblk = -(-A // MOE_BLOCK) + N_EXPERTS
    n_used = bend[-1]
    ar = jnp.arange(nblk, dtype=i32)
    blk_e = jnp.minimum(jnp.searchsorted(bend, ar, side='right'), N_EXPERTS - 1).astype(i32)
    base = start[blk_e] + (ar - (bend - nb_e)[blk_e]) * MOE_BLOCK
    sidx = base[:, None] + jnp.arange(MOE_BLOCK, dtype=i32)[None, :]
    live = (sidx < (start + counts)[blk_e][:, None]) & (ar < n_used)[:, None]
    row_tok = jnp.where(live, order[jnp.clip(sidx, 0, A - 1)] // TOP_K, 0)
    first = ((ar == 0) | (blk_e != jnp.roll(blk_e, 1))) & (ar < n_used)
    slot = (jnp.cumsum(first.astype(i32)) - 1) % 2
    big = i32(nblk)
    nf = lax.cummin(jnp.where(first, ar, big)[::-1])[::-1]
    next_first = jnp.concatenate([nf[1:], big[None]])
    next_e = jnp.where(next_first < big, blk_e[jnp.minimum(next_first, nblk - 1)], -1)
    nt = S // tm_combine
    pos3 = pos_flat.reshape(nt, tm_combine, TOP_K).transpose(0, 2, 1).reshape(nt, 1, TOP_K * tm_combine)
    return dict(row_tok=row_tok.reshape(nblk, 1, MOE_BLOCK), pos3=pos3, blk_e=blk_e,
                first=first.astype(i32), slot=slot.astype(i32), next_e=next_e.astype(i32),
                n_used=n_used.astype(i32).reshape(1))


def kernel(x, mem, mix_norm, even_w_in, conv_w, conv_b, conv_norm_g, conv_norm_b, diff_lambda, diff_subln_g, even_w_out, odd_w_in, sgu_norm_g, sgu_norm_b, sgu_w, sgu_b, swa_sinks, odd_w_out, xa_norm, mem_norm, xa_wq, xa_wkv, xa_wo, moe_norm, router_group_w, router_group_b, router_expert_w, router_expert_b, w_gate, w_up, w_down, final_norm):
    B, S, D = x.shape
    assert B == 1
    row3 = lambda p: p.reshape(p.shape[0], 1, p.shape[1])
    h = x.reshape(S, D)
    mem2 = mem.reshape(mem.shape[1], D)
    hn = rmsnorm(h, row3(mix_norm), 0, BF16)
    for layer in range(DEPTH):
        i = layer // 2
        if layer % 2 == 0:
            proj = matmul([hn], even_w_in, i, out_dtype=BF16)
            y_a = conformer_conv(proj, conv_w, row3(conv_b), row3(conv_norm_g), row3(conv_norm_b), i)
            lambda_init = 0.8 - 0.6 * math.exp(-0.3 * layer)
            y_b = diff_attention(proj, diff_lambda, row3(diff_subln_g), i, lambda_init)
            h = matmul([y_a, y_b], even_w_out, i, res=h, out_dtype=F32)
        else:
            proj = matmul([hn], odd_w_in, i, out_dtype=BF16)
            y_c = spatial_gating(proj, row3(sgu_norm_g), row3(sgu_norm_b), sgu_w,
                                 jnp.transpose(sgu_b, (0, 2, 1)), i)
            y_d = sliding_window_attention(proj, swa_sinks, i)
            h = matmul([y_c, y_d], odd_w_out, i, res=h, out_dtype=F32)
        hq = rmsnorm(h, row3(xa_norm), layer, BF16)
        mem_n = rmsnorm(mem2, row3(mem_norm), layer, BF16)
        q = matmul([hq], xa_wq, layer, out_dtype=BF16)
        kv = matmul([mem_n], xa_wkv, layer, out_dtype=BF16)
        o = cross_attention(q, kv)
        h = matmul([o], xa_wo, layer, res=h, out_dtype=F32)
        last = layer == DEPTH - 1
        moe_args = (h, row3(moe_norm), router_group_w, router_group_b, router_expert_w,
                    router_expert_b, w_gate, w_up, w_down, layer)
        if last:
            (out,) = hierarchical_moe(*moe_args, final_norm.reshape(1, 1, D), 0, F32, False)
        else:
            h, hn = hierarchical_moe(*moe_args, row3(mix_norm), layer + 1, BF16, True)
    return out.reshape(B, S, D)
```

```python
import functools
import math

import jax
import jax.numpy as jnp
from jax import lax
from jax.experimental import pallas as pl
from jax.experimental.pallas import tpu as pltpu

F32 = jnp.float32
BF16 = jnp.bfloat16

D_MODEL = 4096
DEPTH = 2
EPS = 1e-5
BLOCK = 128

CONV_CH = 2048
CONV_WIDTH = 31
CONV_GROUPS = 16
CONV_HALO = 32
CONV_SUB = 64

DIFF_HEADS = 8
DIFF_HD = 128
DIFF_QK = DIFF_HEADS * 2 * DIFF_HD
DIFF_V = DIFF_HEADS * 2 * DIFF_HD

SGU_CH = 2048
SGU_GROUPS = 8
SGU_CHUNK = 128

SWA_HEADS = 32
SWA_KV = 4
SWA_HD = 64
WINDOW = 128

XA_HEADS = 4
XA_HD = 256

N_GROUPS = 8
EXP_PER_GROUP = 8
N_EXPERTS = N_GROUPS * EXP_PER_GROUP
TOP_K = 2
D_FF = 384
MOE_BLOCK = 128
GATHER_AHEAD = 3
W_CHUNKS = 4

LANES = 128
SUBLANES = 8
LOG2E = math.log2(math.e)
VMEM_LIMIT = 56 * 1024 * 1024


def _params(sem, vmem=None):
    return pltpu.CompilerParams(dimension_semantics=sem, vmem_limit_bytes=vmem)


def _rmsnorm_kernel(x_ref, g_ref, o_ref):
    x = x_ref[...].astype(F32)
    ms = jnp.mean(x * x, axis=-1, keepdims=True)
    o_ref[...] = (x * lax.rsqrt(ms + EPS) * g_ref[...]).astype(o_ref.dtype)


def rmsnorm(x, g2, layer, out_dtype, tm=256):
    M, D = x.shape
    return pl.pallas_call(
        _rmsnorm_kernel,
        grid=(M // tm,),
        in_specs=[pl.BlockSpec((tm, D), lambda i: (i, 0)),
                  pl.BlockSpec((None, 1, D), lambda i: (layer, 0, 0))],
        out_specs=pl.BlockSpec((tm, D), lambda i: (i, 0)),
        out_shape=jax.ShapeDtypeStruct((M, D), out_dtype),
        compiler_params=_params(("arbitrary",)),
        name="rmsnorm",
    )(x, g2)


def _mm_kernel(*refs, k_splits, has_res):
    n_a = len(k_splits)
    a_refs = refs[:n_a]
    w_ref = refs[n_a]
    r_ref = refs[n_a + 1] if has_res else None
    o_ref = refs[n_a + 1 + has_res]
    wb_ref = refs[n_a + 2 + has_res]

    @pl.when(pl.program_id(1) == 0)
    def _():
        wb_ref[...] = w_ref[...].astype(BF16)

    acc = None
    off = 0
    for a_ref, ks in zip(a_refs, k_splits):
        part = jnp.dot(a_ref[...], wb_ref[off:off + ks, :], preferred_element_type=F32)
        acc = part if acc is None else acc + part
        off += ks
    if has_res:
        acc = acc + r_ref[...]
    o_ref[...] = acc.astype(o_ref.dtype)


MM_WEIGHT_BLOCK_BYTES = 8 * 1024 * 1024
MM_VMEM_BUDGET = 46 * 1024 * 1024


def _mm_tiles(M, K, N, out_bytes, has_res):
    tn = min(N, max(LANES, MM_WEIGHT_BLOCK_BYTES // (K * 4)))
    assert N % tn == 0
    for tm in (1024, 512, 256, 128):
        need = (2 * tm * K * 2 + 2 * K * tn * 4 + K * tn * 2
                + 2 * tm * tn * out_bytes + (2 * tm * tn * 4 if has_res else 0))
        if tm <= M and M % tm == 0 and need <= MM_VMEM_BUDGET:
            return tm, tn
    raise ValueError("no matmul tiling fits")


def matmul(a_list, w3, layer, *, res=None, out_dtype):
    M = a_list[0].shape[0]
    _, K, N = w3.shape
    k_splits = tuple(a.shape[1] for a in a_list)
    assert sum(k_splits) == K
    tm, tn = _mm_tiles(M, K, N, jnp.dtype(out_dtype).itemsize, res is not None)
    in_specs = [pl.BlockSpec((tm, ks), lambda j, i: (i, 0)) for ks in k_splits]
    in_specs.append(pl.BlockSpec((None, K, tn), lambda j, i: (layer, 0, j)))
    args = list(a_list) + [w3]
    if res is not None:
        in_specs.append(pl.BlockSpec((tm, tn), lambda j, i: (i, j)))
        args.append(res)
    return pl.pallas_call(
        functools.partial(_mm_kernel, k_splits=k_splits, has_res=res is not None),
        grid=(N // tn, M // tm),
        in_specs=in_specs,
        out_specs=pl.BlockSpec((tm, tn), lambda j, i: (i, j)),
        out_shape=jax.ShapeDtypeStruct((M, N), out_dtype),
        scratch_shapes=[pltpu.VMEM((K, tn), BF16)],
        compiler_params=_params(("arbitrary", "arbitrary"), VMEM_LIMIT),
        name="matmul",
    )(*args)


def _conv_kernel(a_ref, g_ref, w_ref, b_ref, ng_ref, nb_ref, o_ref, ybuf, stage, shbuf, *, T, LB):
    i = pl.program_id(1)

    @pl.when(i == 0)
    def _():
        ybuf[0:CONV_HALO, :] = jnp.zeros((CONV_HALO, LB), F32)

    @pl.when(i > 0)
    def _():
        ybuf[0:CONV_HALO, :] = ybuf[T:T + CONV_HALO, :]

    a = a_ref[...].astype(F32)
    g = g_ref[...].astype(F32)
    ybuf[CONV_HALO:CONV_HALO + T, :] = a * jax.nn.sigmoid(g)

    first = CONV_HALO - (CONV_WIDTH - 1)
    n_sub = T // CONV_SUB

    def sub_tile(it, carry):
        r0 = pl.multiple_of(it * CONV_SUB, CONV_SUB)
        stage[...] = ybuf[pl.ds(r0, CONV_SUB + CONV_HALO), :]
        for gi in range(LB // LANES):
            cols = slice(gi * LANES, (gi + 1) * LANES)
            acc = jnp.broadcast_to(b_ref[:, cols], (CONV_SUB, LANES))
            for ph in range(SUBLANES):
                taps = range(ph, CONV_WIDTH, SUBLANES)
                n = CONV_SUB + SUBLANES * (len(taps) - 1)
                sh = shbuf.at[gi, ph % 2]
                sh[0:n, :] = stage[first + ph:first + ph + n, cols]
                for a, j in enumerate(taps):
                    acc = acc + w_ref[j:j + 1, cols] * sh[SUBLANES * a:SUBLANES * a + CONV_SUB, :]
            mu = jnp.mean(acc, axis=-1, keepdims=True)
            d = acc - mu
            var = jnp.mean(d * d, axis=-1, keepdims=True)
            y = d * lax.rsqrt(var + EPS) * ng_ref[:, cols] + nb_ref[:, cols]
            o_ref[pl.ds(r0, CONV_SUB), cols] = (y * jax.nn.sigmoid(y)).astype(o_ref.dtype)
        return carry

    lax.fori_loop(0, n_sub, sub_tile, 0)


def conformer_conv(proj, conv_w, conv_b, norm_g, norm_b, li, T=256, LB=512):
    S = proj.shape[0]
    assert CONV_CH // CONV_GROUPS == LANES
    ncb = CONV_CH // LB
    vec = lambda: pl.BlockSpec((None, 1, LB), lambda c, i: (li, 0, c))
    return pl.pallas_call(
        functools.partial(_conv_kernel, T=T, LB=LB),
        grid=(ncb, S // T),
        in_specs=[pl.BlockSpec((T, LB), lambda c, i: (i, c)),
                  pl.BlockSpec((T, LB), lambda c, i: (i, ncb + c)),
                  pl.BlockSpec((None, CONV_WIDTH, LB), lambda c, i: (li, 0, c)),
                  vec(), vec(), vec()],
        out_specs=pl.BlockSpec((T, LB), lambda c, i: (i, c)),
        out_shape=jax.ShapeDtypeStruct((S, CONV_CH), BF16),
        scratch_shapes=[pltpu.VMEM((CONV_HALO + T, LB), F32),
                        pltpu.VMEM((CONV_HALO + CONV_SUB, LB), F32),
                        pltpu.VMEM((LB // LANES, 2, CONV_HALO + CONV_SUB, LANES), F32)],
        compiler_params=_params(("arbitrary", "arbitrary")),
        name="conformer_conv",
    )(proj, proj, conv_w, conv_b, norm_g, norm_b)


def _diffattn_kernel(q_ref, k_ref, v_ref, lam_ref, g_ref, o_ref, acc_ref, m_ref, l_ref, s0_ref, s1_ref,
                     *, tq, lambda_init):
    qi = pl.program_id(1)
    d = DIFF_HD
    tk = 2 * tq
    c_exp = (d ** -0.5) * LOG2E
    nfull = qi // 2
    acc_ref[...] = jnp.zeros_like(acc_ref)
    m_ref[...] = jnp.full_like(m_ref, -jnp.inf)
    l_ref[...] = jnp.zeros_like(l_ref)

    def qk(t, s_ref):
        kblk = k_ref[pl.ds(pl.multiple_of(t * tk, tk), tk), :]
        for c in range(2):
            s_ref[c] = lax.dot_general(q_ref[:, c * d:(c + 1) * d], kblk[:, c * d:(c + 1) * d],
                                       (((1,), (1,)), ((), ())), preferred_element_type=F32)

    def softmax_pv(t, s_ref, width, masked, row0):
        vblk = v_ref[pl.ds(pl.multiple_of(t * tk, tk), width), :]
        for c in range(2):
            s = s_ref[c, :, 0:width]
            if masked:
                row = lax.broadcasted_iota(jnp.int32, (tq, width), 0) + row0
                col = lax.broadcasted_iota(jnp.int32, (tq, width), 1)
                s = jnp.where(row >= col, s, -jnp.inf)
            m = m_ref[c]
            m_new = jnp.maximum(m, jnp.max(s, axis=-1, keepdims=True))
            alpha = jnp.exp2((m - m_new) * c_exp)
            p = jnp.exp2((s - m_new) * c_exp)
            l_ref[c] = alpha * l_ref[c] + jnp.sum(p, axis=-1, keepdims=True)
            m_ref[c] = m_new
            acc_ref[c] = alpha * acc_ref[c] + jnp.dot(p.astype(BF16), vblk,
                                                      preferred_element_type=F32)

    @pl.when(nfull % 2 == 0)
    def _():
        qk(0, s0_ref)

    @pl.when(nfull % 2 == 1)
    def _():
        qk(0, s1_ref)

    def body(t, carry):
        par = (nfull - t) % 2

        @pl.when(par == 0)
        def _():
            qk(t + 1, s1_ref)
            softmax_pv(t, s0_ref, tk, False, 0)

        @pl.when(par == 1)
        def _():
            qk(t + 1, s0_ref)
            softmax_pv(t, s1_ref, tk, False, 0)
        return carry

    lax.fori_loop(0, nfull, body, 0)

    @pl.when(qi % 2 == 0)
    def _():
        softmax_pv(nfull, s0_ref, tq, True, 0)

    @pl.when(qi % 2 == 1)
    def _():
        softmax_pv(nfull, s0_ref, tk, True, tq)

    lv = lam_ref[...]
    lam = (jnp.exp(jnp.sum(lv[0:1] * lv[1:2], axis=-1, keepdims=True))
           - jnp.exp(jnp.sum(lv[2:3] * lv[3:4], axis=-1, keepdims=True)) + lambda_init)
    o = acc_ref[0] / l_ref[0] - lam * (acc_ref[1] / l_ref[1])
    ms = jnp.mean(o * o, axis=-1, keepdims=True)
    o = o * lax.rsqrt(ms + EPS) * g_ref[...] * (1.0 - lambda_init)
    o_ref[...] = o.astype(o_ref.dtype)


def diff_attention(proj, lam_vecs, subln_g3, li, lambda_init, tq=512):
    S = proj.shape[0]
    assert S % (2 * tq) == 0
    hw = 2 * DIFF_HD
    q0 = 2 * CONV_CH // hw
    k0 = q0 + DIFF_QK // hw
    v0 = k0 + DIFF_QK // hw
    return pl.pallas_call(
        functools.partial(_diffattn_kernel, tq=tq, lambda_init=lambda_init),
        grid=(DIFF_HEADS, S // tq),
        in_specs=[pl.BlockSpec((tq, hw), lambda h, i: (i, q0 + h)),
                  pl.BlockSpec((S, hw), lambda h, i: (0, k0 + h)),
                  pl.BlockSpec((S, hw), lambda h, i: (0, v0 + h)),
                  pl.BlockSpec((None, 4, DIFF_HD), lambda h, i: (li, 0, 0)),
                  pl.BlockSpec((None, 1, hw), lambda h, i: (li, 0, 0))],
        out_specs=pl.BlockSpec((tq, hw), lambda h, i: (i, h)),
        out_shape=jax.ShapeDtypeStruct((S, DIFF_V), BF16),
        scratch_shapes=[pltpu.VMEM((2, tq, hw), F32),
                        pltpu.VMEM((2, tq, 1), F32),
                        pltpu.VMEM((2, tq, 1), F32),
                        pltpu.VMEM((2, tq, 2 * tq), F32),
                        pltpu.VMEM((2, tq, 2 * tq), F32)],
        compiler_params=_params(("arbitrary", "arbitrary"), VMEM_LIMIT),
        name="diff_attention",
    )(proj, proj, proj, lam_vecs, subln_g3)


def _sgu_kernel(u_ref, v_ref, ng_ref, nb_ref, w_ref, bs_ref, o_ref):
    T = SGU_CHUNK
    gw = SGU_CH // SGU_GROUPS
    row = lax.broadcasted_iota(jnp.int32, (T, T), 0)
    col = lax.broadcasted_iota(jnp.int32, (T, T), 1)
    causal = row >= col
    for g in range(SGU_GROUPS):
        cols = slice(g * gw, (g + 1) * gw)
        u = jax.nn.gelu(u_ref[:, cols].astype(F32), approximate=True)
        v = jax.nn.gelu(v_ref[:, cols].astype(F32), approximate=True)
        mu = jnp.mean(v, axis=-1, keepdims=True)
        d = v - mu
        var = jnp.mean(d * d, axis=-1, keepdims=True)
        vn = d * lax.rsqrt(var + EPS) * ng_ref[:, cols] + nb_ref[:, cols]
        w = jnp.where(causal, w_ref[g], 0.0).astype(BF16)
        mixed = jnp.dot(w, vn.astype(BF16), preferred_element_type=F32) + bs_ref[:, g:g + 1]
        o_ref[:, cols] = (u * mixed).astype(o_ref.dtype)


def spatial_gating(proj, norm_g3, norm_b3, w_s, b_s_t, li):
    S = proj.shape[0]
    T = SGU_CHUNK
    vec = lambda: pl.BlockSpec((None, 1, SGU_CH), lambda i: (li, 0, 0))
    return pl.pallas_call(
        _sgu_kernel,
        grid=(S // T,),
        in_specs=[pl.BlockSpec((T, SGU_CH), lambda i: (i, 0)),
                  pl.BlockSpec((T, SGU_CH), lambda i: (i, 1)),
                  vec(), vec(),
                  pl.BlockSpec((None, SGU_GROUPS, T, T), lambda i: (li, 0, 0, 0)),
                  pl.BlockSpec((None, T, SGU_GROUPS), lambda i: (li, 0, 0))],
        out_specs=pl.BlockSpec((T, SGU_CH), lambda i: (i, 0)),
        out_shape=jax.ShapeDtypeStruct((S, SGU_CH), BF16),
        compiler_params=_params(("arbitrary",)),
        name="spatial_gating",
    )(proj, proj, norm_g3, norm_b3, w_s, b_s_t)


def _swa_kernel(sink_ref, q_ref, kp_ref, kc_ref, vp_ref, vc_ref, o_ref, *, li):
    n = pl.program_id(0)
    T = BLOCK
    G = SWA_HEADS // SWA_KV
    d = SWA_HD
    kk = jnp.concatenate([kp_ref[...], kc_ref[...]], axis=0)
    vv = jnp.concatenate([vp_ref[...], vc_ref[...]], axis=0)
    qi = lax.broadcasted_iota(jnp.int32, (T, 2 * T), 0)
    kj = lax.broadcasted_iota(jnp.int32, (T, 2 * T), 1)
    rel = qi + T - kj
    valid = (rel >= 0) & (rel < WINDOW) & ((n * T + kj - T) >= 0)
    scale = d ** -0.5
    for h in range(SWA_KV):
        kh = kk[:, h * d:(h + 1) * d]
        vh = vv[:, h * d:(h + 1) * d]
        outs = []
        for g in range(G):
            c0 = (h * G + g) * d
            s = lax.dot_general(q_ref[:, c0:c0 + d], kh, (((1,), (1,)), ((), ())),
                                preferred_element_type=F32) * scale
            s = jnp.where(valid, s, -jnp.inf)
            sink = sink_ref[li, h * G + g]
            m = jnp.maximum(jnp.max(s, axis=-1, keepdims=True), sink)
            p = jnp.exp(s - m)
            denom = jnp.sum(p, axis=-1, keepdims=True) + jnp.exp(sink - m)
            outs.append(jnp.dot(p.astype(BF16), vh, preferred_element_type=F32) / denom)
        o_ref[:, h * G * d:(h + 1) * G * d] = jnp.concatenate(outs, axis=-1).astype(o_ref.dtype)


def sliding_window_attention(proj, sinks, li):
    S = proj.shape[0]
    T = BLOCK
    qw = SWA_HEADS * SWA_HD
    kw = SWA_KV * SWA_HD
    qb = 2 * SGU_CH // qw
    kb = (2 * SGU_CH + qw) // kw
    vb = kb + 1
    prev = lambda i: jnp.maximum(i - 1, 0)
    return pl.pallas_call(
        functools.partial(_swa_kernel, li=li),
        grid=(S // T,),
        in_specs=[pl.BlockSpec(memory_space=pltpu.SMEM),
                  pl.BlockSpec((T, qw), lambda i: (i, qb)),
                  pl.BlockSpec((T, kw), lambda i: (prev(i), kb)),
                  pl.BlockSpec((T, kw), lambda i: (i, kb)),
                  pl.BlockSpec((T, kw), lambda i: (prev(i), vb)),
                  pl.BlockSpec((T, kw), lambda i: (i, vb))],
        out_specs=pl.BlockSpec((T, qw), lambda i: (i, 0)),
        out_shape=jax.ShapeDtypeStruct((S, qw), BF16),
        compiler_params=_params(("arbitrary",)),
        name="sliding_window_attention",
    )(sinks, proj, proj, proj, proj, proj)


def _xattn_kernel(q_ref, kv_ref, o_ref):
    d = XA_HD
    c_exp = (d ** -0.5) * LOG2E
    for h in range(XA_HEADS):
        q = q_ref[:, h * d:(h + 1) * d]
        k = kv_ref[:, h * d:(h + 1) * d]
        v = kv_ref[:, (XA_HEADS + h) * d:(XA_HEADS + h + 1) * d]
        s = lax.dot_general(q, k, (((1,), (1,)), ((), ())), preferred_element_type=F32)
        m = jnp.max(s, axis=-1, keepdims=True)
        p = jnp.exp2((s - m) * c_exp)
        l = jnp.sum(p, axis=-1, keepdims=True)
        o = jnp.dot(p.astype(BF16), v, preferred_element_type=F32) / l
        o_ref[:, h * d:(h + 1) * d] = o.astype(o_ref.dtype)


def cross_attention(q, kv, tq=512):
    S, qw = q.shape
    M = kv.shape[0]
    return pl.pallas_call(
        _xattn_kernel,
        grid=(S // tq,),
        in_specs=[pl.BlockSpec((tq, qw), lambda i: (i, 0)),
                  pl.BlockSpec((M, 2 * qw), lambda i: (0, 0))],
        out_specs=pl.BlockSpec((tq, qw), lambda i: (i, 0)),
        out_shape=jax.ShapeDtypeStruct((S, qw), BF16),
        compiler_params=_params(("arbitrary",)),
        name="cross_attention",
    )(q, kv)


def _pack_bf16_pair(y):
    c = y.shape[1] // 2
    hi = lax.bitcast_convert_type(y[:, :c].astype(BF16).astype(F32), jnp.uint32)
    lo = lax.bitcast_convert_type(y[:, c:].astype(BF16).astype(F32), jnp.uint32)
    return hi | lax.shift_right_logical(lo, jnp.uint32(16))


def _unpack_bf16_pair(u):
    hi = lax.bitcast_convert_type(u & jnp.uint32(0xFFFF0000), F32)
    lo = lax.bitcast_convert_type(lax.shift_left(u, jnp.uint32(16)), F32)
    return hi, lo


def _router_kernel(h_ref, g_ref, wr_ref, br_ref, hn_ref, route_ref):
    x = h_ref[...]
    ms = jnp.mean(x * x, axis=-1, keepdims=True)
    hn = x * lax.rsqrt(ms + EPS) * g_ref[...]
    hn_ref[...] = _pack_bf16_pair(hn)
    hn_hi = hn.astype(BF16)
    hn_lo = (hn - hn_hi.astype(F32)).astype(BF16)
    z = jnp.dot(hn_hi, wr_ref[...], preferred_element_type=F32)
    logits = (z[:, :LANES] + z[:, LANES:]
              + jnp.dot(hn_lo, wr_ref[:, :LANES], preferred_element_type=F32) + br_ref[...])
    tm = x.shape[0]
    lane = lax.broadcasted_iota(jnp.int32, (tm, LANES), 1)
    big = jnp.int32(2 * LANES)
    gl = jnp.where(lane < N_GROUPS, logits, -jnp.inf)
    gmax = jnp.max(gl, axis=-1, keepdims=True)
    gsel = jnp.min(jnp.where(gl == gmax, lane, big), axis=-1, keepdims=True)
    g_w = 1.0 / jnp.sum(jnp.exp(gl - gmax), axis=-1, keepdims=True)
    lo = N_GROUPS + gsel * EXP_PER_GROUP
    el = jnp.where((lane >= lo) & (lane < lo + EXP_PER_GROUP), logits, -jnp.inf)
    e1 = jnp.max(el, axis=-1, keepdims=True)
    i1 = jnp.min(jnp.where(el == e1, lane, big), axis=-1, keepdims=True)
    el2 = jnp.where(lane == i1, -jnp.inf, el)
    e2 = jnp.max(el2, axis=-1, keepdims=True)
    i2 = jnp.min(jnp.where(el2 == e2, lane, big), axis=-1, keepdims=True)
    t = jnp.exp(e2 - e1)
    w1 = g_w / (1.0 + t)
    w2 = g_w * t / (1.0 + t)
    id1 = (i1 - N_GROUPS).astype(F32)
    id2 = (i2 - N_GROUPS).astype(F32)
    route = jnp.where(lane == 0, id1, jnp.where(lane == 1, id2,
                      jnp.where(lane == 2, w1, jnp.where(lane == 3, w2, 0.0))))
    route_ref[...] = route


def moe_router(h, g3, wr, br, layer, tm=256):
    S, D = h.shape
    return pl.pallas_call(
        _router_kernel,
        grid=(S // tm,),
        in_specs=[pl.BlockSpec((tm, D), lambda i: (i, 0)),
                  pl.BlockSpec((None, 1, D), lambda i: (layer, 0, 0)),
                  pl.BlockSpec((D, 2 * LANES), lambda i: (0, 0)),
                  pl.BlockSpec((1, LANES), lambda i: (0, 0))],
        out_specs=[pl.BlockSpec((tm, D // 2), lambda i: (i, 0)),
                   pl.BlockSpec((tm, LANES), lambda i: (i, 0))],
        out_shape=[jax.ShapeDtypeStruct((S, D // 2), jnp.uint32),
                   jax.ShapeDtypeStruct((S, LANES), F32)],
        compiler_params=_params(("arbitrary",), VMEM_LIMIT),
        name="moe_router",
    )(h, g3, wr, br)


def _dispatch_kernel(nu_ref, *refs):
    tok_first = refs[:GATHER_AHEAD]
    tok_ahead, hn_hbm, o_ref, buf, sem = refs[GATHER_AHEAD:]
    nslot = GATHER_AHEAD + 1
    b = pl.program_id(0)
    nu = nu_ref[0]
    xs = b % nslot

    def row_copy(t, s, r):
        return pltpu.make_async_copy(hn_hbm.at[pl.ds(t, 1), :], buf.at[s, pl.ds(r, 1), :], sem.at[s])

    def start_rows(tok_ref, s):
        def body(j, c):
            row_copy(tok_ref[0, 2 * j], s, 2 * j).start(priority=0)
            row_copy(tok_ref[0, 2 * j + 1], s, 2 * j + 1).start(priority=1)
            return c
        lax.fori_loop(0, MOE_BLOCK // 2, body, 0, unroll=4)

    @pl.when(b == 0)
    def _():
        for k in range(GATHER_AHEAD):
            pl.when(k < nu)(functools.partial(start_rows, tok_first[k], k))

    @pl.when(b + GATHER_AHEAD < nu)
    def _():
        start_rows(tok_ahead, (b + GATHER_AHEAD) % nslot)

    @pl.when(b < nu)
    def _():
        def wait_body(r, c):
            row_copy(0, xs, r).wait()
            return c
        lax.fori_loop(0, MOE_BLOCK, wait_body, 0, unroll=8)
        o_ref[...] = buf[xs]

    @pl.when(b >= nu)
    def _():
        o_ref[...] = jnp.zeros_like(o_ref)


def moe_dispatch(hn_packed, plan):
    S, C = hn_packed.shape
    nblk = plan["row_tok"].shape[0]
    assert nblk > GATHER_AHEAD
    tok_spec = lambda f: pl.BlockSpec((None, 1, MOE_BLOCK), f, memory_space=pltpu.SMEM)
    fixed = [tok_spec(functools.partial(lambda k, b, nu: (k, 0, 0), k)) for k in range(GATHER_AHEAD)]
    ahead = tok_spec(lambda b, nu: (jnp.minimum(b + GATHER_AHEAD, nblk - 1), 0, 0))
    grid_spec = pltpu.PrefetchScalarGridSpec(
        num_scalar_prefetch=1,
        grid=(nblk,),
        in_specs=fixed + [ahead, pl.BlockSpec(memory_space=pl.ANY)],
        out_specs=pl.BlockSpec((MOE_BLOCK, C), lambda b, nu: (b, 0)),
        scratch_shapes=[pltpu.VMEM((GATHER_AHEAD + 1, MOE_BLOCK, C), jnp.uint32),
                        pltpu.SemaphoreType.DMA((GATHER_AHEAD + 1,))],
    )
    return pl.pallas_call(
        _dispatch_kernel,
        grid_spec=grid_spec,
        out_shape=jax.ShapeDtypeStruct((nblk * MOE_BLOCK, C), jnp.uint32),
        compiler_params=_params(("arbitrary",)),
        name="moe_dispatch",
    )(plan["n_used"], *([plan["row_tok"]] * (GATHER_AHEAD + 1)), hn_packed)


def _expert_kernel(be_ref, first_ref, slot_ref, nexte_ref, nu_ref,
                   x_ref, wg_hbm, wu_hbm, wd_hbm,
                   o_ref, wgbuf, wubuf, wdbuf, wsem, *, layer):
    b = pl.program_id(0)
    nu = nu_ref[0]

    def w_copies(e, s):
        out = []
        for m, (src, dst) in enumerate(((wg_hbm, wgbuf), (wu_hbm, wubuf), (wd_hbm, wdbuf))):
            rows = dst.shape[1] // W_CHUNKS
            for k in range(W_CHUNKS):
                rs = pl.ds(k * rows, rows)
                out.append(pltpu.make_async_copy(src.at[layer, e, rs], dst.at[s, rs], wsem.at[s, m]))
        return out

    def start_weights(e, s):
        for k, c in enumerate(w_copies(e, s)):
            c.start(priority=k % 2)

    @pl.when(b == 0)
    def _():
        start_weights(be_ref[0], 0)

    @pl.when(b < nu)
    def _():
        s = slot_ref[b]

        @pl.when(first_ref[b] == 1)
        def _():
            for c in w_copies(be_ref[b], s):
                c.wait()
            ne = nexte_ref[b]

            @pl.when(ne >= 0)
            def _():
                start_weights(ne, 1 - s)

        x_hi, x_lo = _unpack_bf16_pair(x_ref[...])
        half = x_hi.shape[1]
        hg = (jnp.dot(x_hi, wgbuf[s, 0:half, :], preferred_element_type=F32)
              + jnp.dot(x_lo, wgbuf[s, half:, :], preferred_element_type=F32))
        hu = (jnp.dot(x_hi, wubuf[s, 0:half, :], preferred_element_type=F32)
              + jnp.dot(x_lo, wubuf[s, half:, :], preferred_element_type=F32))
        hb = (hg * jax.nn.sigmoid(hg) * hu).astype(BF16).astype(F32)
        o_ref[...] = _pack_bf16_pair(jnp.dot(hb, wdbuf[s], preferred_element_type=F32))

    @pl.when(b >= nu)
    def _():
        o_ref[...] = jnp.zeros_like(o_ref)


def moe_experts(xb, plan, w_gate, w_up, w_down, layer):
    R, C = xb.shape
    D = 2 * C
    nblk = R // MOE_BLOCK
    anyspec = pl.BlockSpec(memory_space=pl.ANY)
    grid_spec = pltpu.PrefetchScalarGridSpec(
        num_scalar_prefetch=5,
        grid=(nblk,),
        in_specs=[pl.BlockSpec((MOE_BLOCK, C), lambda b, *_: (b, 0)),
                  anyspec, anyspec, anyspec],
        out_specs=pl.BlockSpec((MOE_BLOCK, C), lambda b, *_: (b, 0)),
        scratch_shapes=[pltpu.VMEM((2, D, D_FF), F32),
                        pltpu.VMEM((2, D, D_FF), F32),
                        pltpu.VMEM((2, D_FF, D), F32),
                        pltpu.SemaphoreType.DMA((2, 3))],
    )
    return pl.pallas_call(
        functools.partial(_expert_kernel, layer=layer),
        grid_spec=grid_spec,
        out_shape=jax.ShapeDtypeStruct((R, C), jnp.uint32),
        compiler_params=_params(("arbitrary",), VMEM_LIMIT),
        name="moe_experts",
    )(plan["blk_e"], plan["first"], plan["slot"], plan["next_e"], plan["n_used"],
      xb, w_gate, w_up, w_down)


def _combine_kernel(*refs, tm, nt, emit_h):
    pos_first = refs[:GATHER_AHEAD]
    pos_ahead, h_ref, route_ref, g_ref, y_hbm = refs[GATHER_AHEAD:GATHER_AHEAD + 5]
    rest = refs[GATHER_AHEAD + 5:]
    if emit_h:
        h_out_ref, n_out_ref, ybuf, sem = rest
    else:
        n_out_ref, ybuf, sem = rest
    nslot = GATHER_AHEAD + 1
    i = pl.program_id(0)
    xs = i % nslot

    def row_copy(p, s, j):
        return pltpu.make_async_copy(y_hbm.at[pl.ds(p, 1), :], ybuf.at[s, pl.ds(j, 1), :], sem.at[s])

    def start_rows(pos_ref, s):
        def body(j, c):
            row_copy(pos_ref[0, 2 * j], s, 2 * j).start(priority=0)
            row_copy(pos_ref[0, 2 * j + 1], s, 2 * j + 1).start(priority=1)
            return c
        lax.fori_loop(0, TOP_K * tm // 2, body, 0, unroll=4)

    @pl.when(i == 0)
    def _():
        for k in range(GATHER_AHEAD):
            start_rows(pos_first[k], k)

    @pl.when(i + GATHER_AHEAD < nt)
    def _():
        start_rows(pos_ahead, (i + GATHER_AHEAD) % nslot)

    def wait_body(j, c):
        row_copy(0, xs, j).wait()
        return c
    lax.fori_loop(0, TOP_K * tm, wait_body, 0, unroll=8)

    a_hi, a_lo = _unpack_bf16_pair(ybuf[xs, 0:tm, :])
    b_hi, b_lo = _unpack_bf16_pair(ybuf[xs, tm:2 * tm, :])
    w1 = route_ref[:, TOP_K:TOP_K + 1]
    w2 = route_ref[:, TOP_K + 1:TOP_K + 2]
    half = a_hi.shape[1]
    h_lo = h_ref[:, 0:half] + w1 * a_hi + w2 * b_hi
    h_hi = h_ref[:, half:] + w1 * a_lo + w2 * b_lo
    if emit_h:
        h_out_ref[:, 0:half] = h_lo
        h_out_ref[:, half:] = h_hi
    ms = (jnp.sum(h_lo * h_lo, axis=-1, keepdims=True)
          + jnp.sum(h_hi * h_hi, axis=-1, keepdims=True)) / (2 * half)
    r = lax.rsqrt(ms + EPS)
    n_out_ref[:, 0:half] = (h_lo * r * g_ref[:, 0:half]).astype(n_out_ref.dtype)
    n_out_ref[:, half:] = (h_hi * r * g_ref[:, half:]).astype(n_out_ref.dtype)


def moe_combine(h, route, yb, pos3, g3, g_layer, norm_dtype, emit_h, tm=128):
    S, D = h.shape
    nt = S // tm
    assert nt > GATHER_AHEAD
    pos_spec = lambda f: pl.BlockSpec((None, 1, TOP_K * tm), f, memory_space=pltpu.SMEM)
    out_specs = [pl.BlockSpec((tm, D), lambda i: (i, 0))]
    out_shape = [jax.ShapeDtypeStruct((S, D), norm_dtype)]
    if emit_h:
        out_specs = [pl.BlockSpec((tm, D), lambda i: (i, 0))] + out_specs
        out_shape = [jax.ShapeDtypeStruct((S, D), F32)] + out_shape
    return pl.pallas_call(
        functools.partial(_combine_kernel, tm=tm, nt=nt, emit_h=emit_h),
        grid=(nt,),
        in_specs=[pos_spec(functools.partial(lambda k, i: (k, 0, 0), k)) for k in range(GATHER_AHEAD)] + [
                  pos_spec(lambda i: (jnp.minimum(i + GATHER_AHEAD, nt - 1), 0, 0)),
                  pl.BlockSpec((tm, D), lambda i: (i, 0)),
                  pl.BlockSpec((tm, LANES), lambda i: (i, 0)),
                  pl.BlockSpec((None, 1, D), lambda i: (g_layer, 0, 0)),
                  pl.BlockSpec(memory_space=pl.ANY)],
        out_specs=out_specs,
        out_shape=out_shape,
        scratch_shapes=[pltpu.VMEM((GATHER_AHEAD + 1, TOP_K * tm, D // 2), jnp.uint32),
                        pltpu.SemaphoreType.DMA((GATHER_AHEAD + 1,))],
        compiler_params=_params(("arbitrary",), VMEM_LIMIT),
        name="moe_combine",
    )(*([pos3] * (GATHER_AHEAD + 1)), h, route, g3, yb)


def hierarchical_moe(h, g3, wr_g, br_g, wr_e, br_e, w_gate, w_up, w_down, layer,
                     next_g3, next_layer, next_dtype, emit_h, tm_combine=128):
    S, D = h.shape
    wr = jnp.concatenate([wr_g[layer], jnp.transpose(wr_e[layer], (1, 0, 2)).reshape(D, N_EXPERTS)],
                         axis=1)
    wr = jnp.pad(wr, ((0, 0), (0, LANES - wr.shape[1])))
    wr_hi = wr.astype(BF16)
    wr = jnp.concatenate([wr_hi, (wr - wr_hi.astype(F32)).astype(BF16)], axis=1)
    br =jnp.concatenate([br_g[layer], br_e[layer].reshape(-1)])
    br = jnp.pad(br, (0, LANES - br.shape[0])).reshape(1, LANES)
    hn_packed, route = moe_router(h, g3, wr, br, layer)
    plan = _moe_plan(route[:, 0:TOP_K].astype(jnp.int32), tm_combine)
    yb = moe_experts(moe_dispatch(hn_packed, plan), plan, w_gate, w_up, w_down, layer)
    return moe_combine(h, route, yb, plan["pos3"], next_g3, next_layer, next_dtype, emit_h,
                       tm=tm_combine)


def _moe_plan(expert_id, tm_combine):
    S = expert_id.shape[0]
    A = S * TOP_K
    i32 = jnp.int32
    flat_e = expert_id.reshape(-1)
    order = jnp.argsort(flat_e).astype(i32)
    rank = jnp.argsort(order).astype(i32)
    edges = jnp.searchsorted(flat_e[order], jnp.arange(N_EXPERTS + 1, dtype=i32), side='left').astype(i32)
    start, counts = edges[:-1], edges[1:] - edges[:-1]
    nb_e = (counts + MOE_BLOCK - 1) // MOE_BLOCK
    bend = jnp.cumsum(nb_e)
    pstart = (bend - nb_e) * MOE_BLOCK
    shift = pstart - start
    dshift = shift - jnp.concatenate([jnp.zeros((1,), i32), shift[:-1]])
    pos_flat = rank + jnp.sum(jnp.where(rank[:, None] >= start[None, :], dshift[None, :], 0), axis=1)
    nblk = -(-A // MOE_BLOCK) + N_EXPERTS
    n_used = bend[-1]
    ar = jnp.arange(nblk, dtype=i32)
    blk_e = jnp.minimum(jnp.searchsorted(bend, ar, side='right'), N_EXPERTS - 1).astype(i32)
    base = start[blk_e] + (ar - (bend - nb_e)[blk_e]) * MOE_BLOCK
    sidx = base[:, None] + jnp.arange(MOE_BLOCK, dtype=i32)[None, :]
    live = (sidx < (start + counts)[blk_e][:, None]) & (ar < n_used)[:, None]
    row_tok = jnp.where(live, order[jnp.clip(sidx, 0, A - 1)] // TOP_K, 0)
    first = ((ar == 0) | (blk_e != jnp.roll(blk_e, 1))) & (ar < n_used)
    slot = (jnp.cumsum(first.astype(i32)) - 1) % 2
    big = i32(nblk)
    nf = lax.cummin(jnp.where(first, ar, big)[::-1])[::-1]
    next_first = jnp.concatenate([nf[1:], big[None]])
    next_e = jnp.where(next_first < big, blk_e[jnp.minimum(next_first, nblk - 1)], -1)
    nt = S // tm_combine
    pos3 = pos_flat.reshape(nt, tm_combine, TOP_K).transpose(0, 2, 1).reshape(nt, 1, TOP_K * tm_combine)
    return dict(row_tok=row_tok.reshape(nblk, 1, MOE_BLOCK), pos3=pos3, blk_e=blk_e,
                first=first.astype(i32), slot=slot.astype(i32), next_e=next_e.astype(i32),
                n_used=n_used.astype(i32).reshape(1))


def kernel(x, mem, mix_norm, even_w_in, conv_w, conv_b, conv_norm_g, conv_norm_b, diff_lambda, diff_subln_g, even_w_out, odd_w_in, sgu_norm_g, sgu_norm_b, sgu_w, sgu_b, swa_sinks, odd_w_out, xa_norm, mem_norm, xa_wq, xa_wkv, xa_wo, moe_norm, router_group_w, router_group_b, router_expert_w, router_expert_b, w_gate, w_up, w_down, final_norm):
    B, S, D = x.shape
    assert B == 1
    row3 = lambda p: p.reshape(p.shape[0], 1, p.shape[1])
    h = x.reshape(S, D)
    mem2 = mem.reshape(mem.shape[1], D)
    hn = rmsnorm(h, row3(mix_norm), 0, BF16)
    for layer in range(DEPTH):
        i = layer // 2
        if layer % 2 == 0:
            proj = matmul([hn], even_w_in, i, out_dtype=BF16)
            y_a = conformer_conv(proj, conv_w, row3(conv_b), row3(conv_norm_g), row3(conv_norm_b), i)
            lambda_init = 0.8 - 0.6 * math.exp(-0.3 * layer)
            y_b = diff_attention(proj, diff_lambda, row3(diff_subln_g), i, lambda_init)
            h = matmul([y_a, y_b], even_w_out, i, res=h, out_dtype=F32)
        else:
            proj = matmul([hn], odd_w_in, i, out_dtype=BF16)
            y_c = spatial_gating(proj, row3(sgu_norm_g), row3(sgu_norm_b), sgu_w,
                                 jnp.transpose(sgu_b, (0, 2, 1)), i)
            y_d = sliding_window_attention(proj, swa_sinks, i)
            h = matmul([y_c, y_d], odd_w_out, i, res=h, out_dtype=F32)
        hq = rmsnorm(h, row3(xa_norm), layer, BF16)
        mem_n = rmsnorm(mem2, row3(mem_norm), layer, BF16)
        q = matmul([hq], xa_wq, layer, out_dtype=BF16)
        kv = matmul([mem_n], xa_wkv, layer, out_dtype=BF16)
        o = cross_attention(q, kv)
        h = matmul([o], xa_wo, layer, res=h, out_dtype=F32)
        last = layer == DEPTH - 1
        moe_args = (h, row3(moe_norm), router_group_w, router_group_b, router_expert_w,
                    router_expert_b, w_gate, w_up, w_down, layer)
        if last:
            (out,) = hierarchical_moe(*moe_args, final_norm.reshape(1, 1, D), 0, F32, False)
        else:
            h, hn = hierarchical_moe(*moe_args, row3(mix_norm), layer + 1, BF16, True)
    return out.reshape(B, S, D)
```

```python
import functools
import math

import jax
import jax.numpy as jnp
from jax import lax
from jax.experimental import pallas as pl
from jax.experimental.pallas import tpu as pltpu

F32 = jnp.float32
BF16 = jnp.bfloat16

D_MODEL = 4096
DEPTH = 2
EPS = 1e-5
BLOCK = 128

CONV_CH = 2048
CONV_WIDTH = 31
CONV_GROUPS = 16
CONV_HALO = 32
CONV_SUB = 64

DIFF_HEADS = 8
DIFF_HD = 128
DIFF_QK = DIFF_HEADS * 2 * DIFF_HD
DIFF_V = DIFF_HEADS * 2 * DIFF_HD

SGU_CH = 2048
SGU_GROUPS = 8
SGU_CHUNK = 128

SWA_HEADS = 32
SWA_KV = 4
SWA_HD = 64
WINDOW = 128

XA_HEADS = 4
XA_HD = 256

N_GROUPS = 8
EXP_PER_GROUP = 8
N_EXPERTS = N_GROUPS * EXP_PER_GROUP
TOP_K = 2
D_FF = 384
MOE_BLOCK = 128
GATHER_AHEAD = 3
W_CHUNKS = 4

LANES = 128
SUBLANES = 8
LOG2E = math.log2(math.e)
VMEM_LIMIT = 56 * 1024 * 1024


def _params(sem, vmem=None):
    return pltpu.CompilerParams(dimension_semantics=sem, vmem_limit_bytes=vmem)


def _rmsnorm_kernel(x_ref, g_ref, o_ref):
    x = x_ref[...].astype(F32)
    ms = jnp.mean(x * x, axis=-1, keepdims=True)
    o_ref[...] = (x * lax.rsqrt(ms + EPS) * g_ref[...]).astype(o_ref.dtype)


def rmsnorm(x, g2, layer, out_dtype, tm=256):
    M, D = x.shape
    return pl.pallas_call(
        _rmsnorm_kernel,
        grid=(M // tm,),
        in_specs=[pl.BlockSpec((tm, D), lambda i: (i, 0)),
                  pl.BlockSpec((None, 1, D), lambda i: (layer, 0, 0))],
        out_specs=pl.BlockSpec((tm, D), lambda i: (i, 0)),
        out_shape=jax.ShapeDtypeStruct((M, D), out_dtype),
        compiler_params=_params(("arbitrary",)),
        name="rmsnorm",
    )(x, g2)


def _mm_kernel(*refs, k_splits, has_res):
    n_a = len(k_splits)
    a_refs = refs[:n_a]
    w_ref = refs[n_a]
    r_ref = refs[n_a + 1] if has_res else None
    o_ref = refs[n_a + 1 + has_res]
    wb_ref = refs[n_a + 2 + has_res]

    @pl.when(pl.program_id(1) == 0)
    def _():
        wb_ref[...] = w_ref[...].astype(BF16)

    acc = None
    off = 0
    for a_ref, ks in zip(a_refs, k_splits):
        part = jnp.dot(a_ref[...], wb_ref[off:off + ks, :], preferred_element_type=F32)
        acc = part if acc is None else acc + part
        off += ks
    if has_res:
        acc = acc + r_ref[...]
    o_ref[...] = acc.astype(o_ref.dtype)


MM_WEIGHT_BLOCK_BYTES = 8 * 1024 * 1024
MM_VMEM_BUDGET = 46 * 1024 * 1024


def _mm_tiles(M, K, N, out_bytes, has_res):
    tn = min(N, max(LANES, MM_WEIGHT_BLOCK_BYTES // (K * 4)))
    assert N % tn == 0
    for tm in (1024, 512, 256, 128):
        need = (2 * tm * K * 2 + 2 * K * tn * 4 + K * tn * 2
                + 2 * tm * tn * out_bytes + (2 * tm * tn * 4 if has_res else 0))
        if tm <= M and M % tm == 0 and need <= MM_VMEM_BUDGET:
            return tm, tn
    raise ValueError("no matmul tiling fits")


def matmul(a_list, w3, layer, *, res=None, out_dtype):
    M = a_list[0].shape[0]
    _, K, N = w3.shape
    k_splits = tuple(a.shape[1] for a in a_list)
    assert sum(k_splits) == K
    tm, tn = _mm_tiles(M, K, N, jnp.dtype(out_dtype).itemsize, res is not None)
    in_specs = [pl.BlockSpec((tm, ks), lambda j, i: (i, 0)) for ks in k_splits]
    in_specs.append(pl.BlockSpec((None, K, tn), lambda j, i: (layer, 0, j)))
    args = list(a_list) + [w3]
    if res is not None:
        in_specs.append(pl.BlockSpec((tm, tn), lambda j, i: (i, j)))
        args.append(res)
    return pl.pallas_call(
        functools.partial(_mm_kernel, k_splits=k_splits, has_res=res is not None),
        grid=(N // tn, M // tm),
        in_specs=in_specs,
        out_specs=pl.BlockSpec((tm, tn), lambda j, i: (i, j)),
        out_shape=jax.ShapeDtypeStruct((M, N), out_dtype),
        scratch_shapes=[pltpu.VMEM((K, tn), BF16)],
        compiler_params=_params(("arbitrary", "arbitrary"), VMEM_LIMIT),
        name="matmul",
    )(*args)


def _conv_kernel(a_ref, g_ref, w_ref, b_ref, ng_ref, nb_ref, o_ref, ybuf, stage, shbuf, *, T, LB):
    i = pl.program_id(1)

    @pl.when(i == 0)
    def _():
        ybuf[0:CONV_HALO, :] = jnp.zeros((CONV_HALO, LB), F32)

    @pl.when(i > 0)
    def _():
        ybuf[0:CONV_HALO, :] = ybuf[T:T + CONV_HALO, :]

    a = a_ref[...].astype(F32)
    g = g_ref[...].astype(F32)
    ybuf[CONV_HALO:CONV_HALO + T, :] = a * jax.nn.sigmoid(g)

    first = CONV_HALO - (CONV_WIDTH - 1)
    n_sub = T // CONV_SUB

    def sub_tile(it, carry):
        r0 = pl.multiple_of(it * CONV_SUB, CONV_SUB)
        stage[...] = ybuf[pl.ds(r0, CONV_SUB + CONV_HALO), :]
        for gi in range(LB // LANES):
            cols = slice(gi * LANES, (gi + 1) * LANES)
            acc = jnp.broadcast_to(b_ref[:, cols], (CONV_SUB, LANES))
            for ph in range(SUBLANES):
                taps = range(ph, CONV_WIDTH, SUBLANES)
                n = CONV_SUB + SUBLANES * (len(taps) - 1)
                sh = shbuf.at[gi, ph % 2]
                sh[0:n, :] = stage[first + ph:first + ph + n, cols]
                for a, j in enumerate(taps):
                    acc = acc + w_ref[j:j + 1, cols] * sh[SUBLANES * a:SUBLANES * a + CONV_SUB, :]
            mu = jnp.mean(acc, axis=-1, keepdims=True)
            d = acc - mu
            var = jnp.mean(d * d, axis=-1, keepdims=True)
            y = d * lax.rsqrt(var + EPS) * ng_ref[:, cols] + nb_ref[:, cols]
            o_ref[pl.ds(r0, CONV_SUB), cols] = (y * jax.nn.sigmoid(y)).astype(o_ref.dtype)
        return carry

    lax.fori_loop(0, n_sub, sub_tile, 0)


def conformer_conv(proj, conv_w, conv_b, norm_g, norm_b, li, T=256, LB=512):
    S = proj.shape[0]
    assert CONV_CH // CONV_GROUPS == LANES
    ncb = CONV_CH // LB
    vec = lambda: pl.BlockSpec((None, 1, LB), lambda c, i: (li, 0, c))
    return pl.pallas_call(
        functools.partial(_conv_kernel, T=T, LB=LB),
        grid=(ncb, S // T),
        in_specs=[pl.BlockSpec((T, LB), lambda c, i: (i, c)),
                  pl.BlockSpec((T, LB), lambda c, i: (i, ncb + c)),
                  pl.BlockSpec((None, CONV_WIDTH, LB), lambda c, i: (li, 0, c)),
                  vec(), vec(), vec()],
        out_specs=pl.BlockSpec((T, LB), lambda c, i: (i, c)),
        out_shape=jax.ShapeDtypeStruct((S, CONV_CH), BF16),
        scratch_shapes=[pltpu.VMEM((CONV_HALO + T, LB), F32),
                        pltpu.VMEM((CONV_HALO + CONV_SUB, LB), F32),
                        pltpu.VMEM((LB // LANES, 2, CONV_HALO + CONV_SUB, LANES), F32)],
        compiler_params=_params(("arbitrary", "arbitrary")),
        name="conformer_conv",
    )(proj, proj, conv_w, conv_b, norm_g, norm_b)


def _diffattn_kernel(q_ref, k_ref, v_ref, lam_ref, g_ref, o_ref, acc_ref, m_ref, l_ref, s0_ref, s1_ref,
                     *, tq, lambda_init):
    qi = pl.program_id(1)
    d = DIFF_HD
    tk = 2 * tq
    c_exp = (d ** -0.5) * LOG2E
    nfull = qi // 2
    acc_ref[...] = jnp.zeros_like(acc_ref)
    m_ref[...] = jnp.full_like(m_ref, -jnp.inf)
    l_ref[...] = jnp.zeros_like(l_ref)

    def qk(t, s_ref):
        kblk = k_ref[pl.ds(pl.multiple_of(t * tk, tk), tk), :]
        for c in range(2):
            s_ref[c] = lax.dot_general(q_ref[:, c * d:(c + 1) * d], kblk[:, c * d:(c + 1) * d],
                                       (((1,), (1,)), ((), ())), preferred_element_type=F32)

    def softmax_pv(t, s_ref, width, masked, row0):
        vblk = v_ref[pl.ds(pl.multiple_of(t * tk, tk), width), :]
        for c in range(2):
            s = s_ref[c, :, 0:width]
            if masked:
                row = lax.broadcasted_iota(jnp.int32, (tq, width), 0) + row0
                col = lax.broadcasted_iota(jnp.int32, (tq, width), 1)
                s = jnp.where(row >= col, s, -jnp.inf)
            m = m_ref[c]
            m_new = jnp.maximum(m, jnp.max(s, axis=-1, keepdims=True))
            alpha = jnp.exp2((m - m_new) * c_exp)
            p = jnp.exp2((s - m_new) * c_exp)
            l_ref[c] = alpha * l_ref[c] + jnp.sum(p, axis=-1, keepdims=True)
            m_ref[c] = m_new
            acc_ref[c] = alpha * acc_ref[c] + jnp.dot(p.astype(BF16), vblk,
                                                      preferred_element_type=F32)

    @pl.when(nfull % 2 == 0)
    def _():
        qk(0, s0_ref)

    @pl.when(nfull % 2 == 1)
    def _():
        qk(0, s1_ref)

    def body(t, carry):
        par = (nfull - t) % 2

        @pl.when(par == 0)
        def _():
            qk(t + 1, s1_ref)
            softmax_pv(t, s0_ref, tk, False, 0)

        @pl.when(par == 1)
        def _():
            qk(t + 1, s0_ref)
            softmax_pv(t, s1_ref, tk, False, 0)
        return carry

    lax.fori_loop(0, nfull, body, 0)

    @pl.when(qi % 2 == 0)
    def _():
        softmax_pv(nfull, s0_ref, tq, True, 0)

    @pl.when(qi % 2 == 1)
    def _():
        softmax_pv(nfull, s0_ref, tk, True, tq)

    lv = lam_ref[...]
    lam = (jnp.exp(jnp.sum(lv[0:1] * lv[1:2], axis=-1, keepdims=True))
           - jnp.exp(jnp.sum(lv[2:3] * lv[3:4], axis=-1, keepdims=True)) + lambda_init)
    o = acc_ref[0] / l_ref[0] - lam * (acc_ref[1] / l_ref[1])
    ms = jnp.mean(o * o, axis=-1, keepdims=True)
    o = o * lax.rsqrt(ms + EPS) * g_ref[...] * (1.0 - lambda_init)
    o_ref[...] = o.astype(o_ref.dtype)


def diff_attention(proj, lam_vecs, subln_g3, li, lambda_init, tq=512):
    S = proj.shape[0]
    assert S % (2 * tq) == 0
    hw = 2 * DIFF_HD
    q0 = 2 * CONV_CH // hw
    k0 = q0 + DIFF_QK // hw
    v0 = k0 + DIFF_QK // hw
    return pl.pallas_call(
        functools.partial(_diffattn_kernel, tq=tq, lambda_init=lambda_init),
        grid=(DIFF_HEADS, S // tq),
        in_specs=[pl.BlockSpec((tq, hw), lambda h, i: (i, q0 + h)),
                  pl.BlockSpec((S, hw), lambda h, i: (0, k0 + h)),
                  pl.BlockSpec((S, hw), lambda h, i: (0, v0 + h)),
                  pl.BlockSpec((None, 4, DIFF_HD), lambda h, i: (li, 0, 0)),
                  pl.BlockSpec((None, 1, hw), lambda h, i: (li, 0, 0))],
        out_specs=pl.BlockSpec((tq, hw), lambda h, i: (i, h)),
        out_shape=jax.ShapeDtypeStruct((S, DIFF_V), BF16),
        scratch_shapes=[pltpu.VMEM((2, tq, hw), F32),
                        pltpu.VMEM((2, tq, 1), F32),
                        pltpu.VMEM((2, tq, 1), F32),
                        pltpu.VMEM((2, tq, 2 * tq), F32),
                        pltpu.VMEM((2, tq, 2 * tq), F32)],
        compiler_params=_params(("arbitrary", "arbitrary"), VMEM_LIMIT),
        name="diff_attention",
    )(proj, proj, proj, lam_vecs, subln_g3)


def _sgu_kernel(u_ref, v_ref, ng_ref, nb_ref, w_ref, bs_ref, o_ref):
    T = SGU_CHUNK
    gw = SGU_CH // SGU_GROUPS
    row = lax.broadcasted_iota(jnp.int32, (T, T), 0)
    col = lax.broadcasted_iota(jnp.int32, (T, T), 1)
    causal = row >= col
    for g in range(SGU_GROUPS):
        cols = slice(g * gw, (g + 1) * gw)
        u = jax.nn.gelu(u_ref[:, cols].astype(F32), approximate=True)
        v = jax.nn.gelu(v_ref[:, cols].astype(F32), approximate=True)
        mu = jnp.mean(v, axis=-1, keepdims=True)
        d = v - mu
        var = jnp.mean(d * d, axis=-1, keepdims=True)
        vn = d * lax.rsqrt(var + EPS) * ng_ref[:, cols] + nb_ref[:, cols]
        w = jnp.where(causal, w_ref[g], 0.0).astype(BF16)
        mixed = jnp.dot(w, vn.astype(BF16), preferred_element_type=F32) + bs_ref[:, g:g + 1]
        o_ref[:, cols] = (u * mixed).astype(o_ref.dtype)


def spatial_gating(proj, norm_g3, norm_b3, w_s, b_s_t, li):
    S = proj.shape[0]
    T = SGU_CHUNK
    vec = lambda: pl.BlockSpec((None, 1, SGU_CH), lambda i: (li, 0, 0))
    return pl.pallas_call(
        _sgu_kernel,
        grid=(S // T,),
        in_specs=[pl.BlockSpec((T, SGU_CH), lambda i: (i, 0)),
                  pl.BlockSpec((T, SGU_CH), lambda i: (i, 1)),
                  vec(), vec(),
                  pl.BlockSpec((None, SGU_GROUPS, T, T), lambda i: (li, 0, 0, 0)),
                  pl.BlockSpec((None, T, SGU_GROUPS), lambda i: (li, 0, 0))],
        out_specs=pl.BlockSpec((T, SGU_CH), lambda i: (i, 0)),
        out_shape=jax.ShapeDtypeStruct((S, SGU_CH), BF16),
        compiler_params=_params(("arbitrary",)),
        name="spatial_gating",
    )(proj, proj, norm_g3, norm_b3, w_s, b_s_t)


def _swa_kernel(sink_ref, q_ref, kp_ref, kc_ref, vp_ref, vc_ref, o_ref, *, li):
    n = pl.program_id(0)
    T = BLOCK
    G = SWA_HEADS // SWA_KV
    d = SWA_HD
    kk = jnp.concatenate([kp_ref[...], kc_ref[...]], axis=0)
    vv = jnp.concatenate([vp_ref[...], vc_ref[...]], axis=0)
    qi = lax.broadcasted_iota(jnp.int32, (T, 2 * T), 0)
    kj = lax.broadcasted_iota(jnp.int32, (T, 2 * T), 1)
    rel = qi + T - kj
    valid = (rel >= 0) & (rel < WINDOW) & ((n * T + kj - T) >= 0)
    scale = d ** -0.5
    for h in range(SWA_KV):
        kh = kk[:, h * d:(h + 1) * d]
        vh = vv[:, h * d:(h + 1) * d]
        outs = []
        for g in range(G):
            c0 = (h * G + g) * d
            s = lax.dot_general(q_ref[:, c0:c0 + d], kh, (((1,), (1,)), ((), ())),
                                preferred_element_type=F32) * scale
            s = jnp.where(valid, s, -jnp.inf)
            sink = sink_ref[li, h * G + g]
            m = jnp.maximum(jnp.max(s, axis=-1, keepdims=True), sink)
            p = jnp.exp(s - m)
            denom = jnp.sum(p, axis=-1, keepdims=True) + jnp.exp(sink - m)
            outs.append(jnp.dot(p.astype(BF16), vh, preferred_element_type=F32) / denom)
        o_ref[:, h * G * d:(h + 1) * G * d] = jnp.concatenate(outs, axis=-1).astype(o_ref.dtype)


def sliding_window_attention(proj, sinks, li):
    S = proj.shape[0]
    T = BLOCK
    qw = SWA_HEADS * SWA_HD
    kw = SWA_KV * SWA_HD
    qb = 2 * SGU_CH // qw
    kb = (2 * SGU_CH + qw) // kw
    vb = kb + 1
    prev = lambda i: jnp.maximum(i - 1, 0)
    return pl.pallas_call(
        functools.partial(_swa_kernel, li=li),
        grid=(S // T,),
        in_specs=[pl.BlockSpec(memory_space=pltpu.SMEM),
                  pl.BlockSpec((T, qw), lambda i: (i, qb)),
                  pl.BlockSpec((T, kw), lambda i: (prev(i), kb)),
                  pl.BlockSpec((T, kw), lambda i: (i, kb)),
                  pl.BlockSpec((T, kw), lambda i: (prev(i), vb)),
                  pl.BlockSpec((T, kw), lambda i: (i, vb))],
        out_specs=pl.BlockSpec((T, qw), lambda i: (i, 0)),
        out_shape=jax.ShapeDtypeStruct((S, qw), BF16),
        compiler_params=_params(("arbitrary",)),
        name="sliding_window_attention",
    )(sinks, proj, proj, proj, proj, proj)


def _xattn_kernel(q_ref, kv_ref, o_ref):
    d = XA_HD
    c_exp = (d ** -0.5) * LOG2E
    for h in range(XA_HEADS):
        q = q_ref[:, h * d:(h + 1) * d]
        k = kv_ref[:, h * d:(h + 1) * d]
        v = kv_ref[:, (XA_HEADS + h) * d:(XA_HEADS + h + 1) * d]
        s = lax.dot_general(q, k, (((1,), (1,)), ((), ())), preferred_element_type=F32)
        m = jnp.max(s, axis=-1, keepdims=True)
        p = jnp.exp2((s - m) * c_exp)
        l = jnp.sum(p, axis=-1, keepdims=True)
        o = jnp.dot(p.astype(BF16), v, preferred_element_type=F32) / l
        o_ref[:, h * d:(h + 1) * d] = o.astype(o_ref.dtype)


def cross_attention(q, kv, tq=512):
    S, qw = q.shape
    M = kv.shape[0]
    return pl.pallas_call(
        _xattn_kernel,
        grid=(S // tq,),
        in_specs=[pl.BlockSpec((tq, qw), lambda i: (i, 0)),
                  pl.BlockSpec((M, 2 * qw), lambda i: (0, 0))],
        out_specs=pl.BlockSpec((tq, qw), lambda i: (i, 0)),
        out_shape=jax.ShapeDtypeStruct((S, qw), BF16),
        compiler_params=_params(("arbitrary",)),
        name="cross_attention",
    )(q, kv)


def _pack_bf16_pair(y):
    c = y.shape[1] // 2
    hi = lax.bitcast_convert_type(y[:, :c].astype(BF16).astype(F32), jnp.uint32)
    lo = lax.bitcast_convert_type(y[:, c:].astype(BF16).astype(F32), jnp.uint32)
    return hi | lax.shift_right_logical(lo, jnp.uint32(16))


def _unpack_bf16_pair(u):
    hi = lax.bitcast_convert_type(u & jnp.uint32(0xFFFF0000), F32)
    lo = lax.bitcast_convert_type(lax.shift_left(u, jnp.uint32(16)), F32)
    return hi, lo


def _router_kernel(h_ref, g_ref, wr_ref, br_ref, hn_ref, route_ref):
    x = h_ref[...]
    ms = jnp.mean(x * x, axis=-1, keepdims=True)
    hn = x * lax.rsqrt(ms + EPS) * g_ref[...]
    packed = _pack_bf16_pair(hn)
    for t in range(hn_ref.shape[1]):
        hn_ref[:, t, :] = packed[:, t * LANES:(t + 1) * LANES]
    hn_hi = hn.astype(BF16)
    hn_lo = (hn - hn_hi.astype(F32)).astype(BF16)
    z = jnp.dot(hn_hi, wr_ref[...], preferred_element_type=F32)
    logits = (z[:, :LANES] + z[:, LANES:]
              + jnp.dot(hn_lo, wr_ref[:, :LANES], preferred_element_type=F32) + br_ref[...])
    tm = x.shape[0]
    lane = lax.broadcasted_iota(jnp.int32, (tm, LANES), 1)
    big = jnp.int32(2 * LANES)
    gl = jnp.where(lane < N_GROUPS, logits, -jnp.inf)
    gmax = jnp.max(gl, axis=-1, keepdims=True)
    gsel = jnp.min(jnp.where(gl == gmax, lane, big), axis=-1, keepdims=True)
    g_w = 1.0 / jnp.sum(jnp.exp(gl - gmax), axis=-1, keepdims=True)
    lo = N_GROUPS + gsel * EXP_PER_GROUP
    el = jnp.where((lane >= lo) & (lane < lo + EXP_PER_GROUP), logits, -jnp.inf)
    e1 = jnp.max(el, axis=-1, keepdims=True)
    i1 = jnp.min(jnp.where(el == e1, lane, big), axis=-1, keepdims=True)
    el2 = jnp.where(lane == i1, -jnp.inf, el)
    e2 = jnp.max(el2, axis=-1, keepdims=True)
    i2 = jnp.min(jnp.where(el2 == e2, lane, big), axis=-1, keepdims=True)
    t = jnp.exp(e2 - e1)
    w1 = g_w / (1.0 + t)
    w2 = g_w * t / (1.0 + t)
    id1 = (i1 - N_GROUPS).astype(F32)
    id2 = (i2 - N_GROUPS).astype(F32)
    route = jnp.where(lane == 0, id1, jnp.where(lane == 1, id2,
                      jnp.where(lane == 2, w1, jnp.where(lane == 3, w2, 0.0))))
    route_ref[...] = route


def moe_router(h, g3, wr, br, layer, tm=256):
    S, D = h.shape
    return pl.pallas_call(
        _router_kernel,
        grid=(S // tm,),
        in_specs=[pl.BlockSpec((tm, D), lambda i: (i, 0)),
                  pl.BlockSpec((None, 1, D), lambda i: (layer, 0, 0)),
                  pl.BlockSpec((D, 2 * LANES), lambda i: (0, 0)),
                  pl.BlockSpec((1, LANES), lambda i: (0, 0))],
        out_specs=[pl.BlockSpec((tm, D // 2 // LANES, LANES), lambda i: (i, 0, 0)),
                   pl.BlockSpec((tm, LANES), lambda i: (i, 0))],
        out_shape=[jax.ShapeDtypeStruct((S, D // 2 // LANES, LANES), jnp.uint32),
                   jax.ShapeDtypeStruct((S, LANES), F32)],
        compiler_params=_params(("arbitrary",), VMEM_LIMIT),
        name="moe_router",
    )(h, g3, wr, br)


def _expert_kernel(be_ref, first_ref, slot_ref, nexte_ref, nu_ref, *refs, layer):
    tok_first = refs[:GATHER_AHEAD]
    (tok_ahead, hn_hbm, wg_hbm, wu_hbm, wd_hbm,
     o_ref, xbuf, wgbuf, wubuf, wdbuf, xsem, wsem) = refs[GATHER_AHEAD:]
    nslot = GATHER_AHEAD + 1
    b = pl.program_id(0)
    nu = nu_ref[0]
    xs = b % nslot

    def row_copy(t, s, r):
        return pltpu.make_async_copy(hn_hbm.at[pl.ds(t, 1)], xbuf.at[s, pl.ds(r, 1)], xsem.at[s])

    def start_rows(tok_ref, s):
        def body(r, c):
            row_copy(tok_ref[0, r], s, r).start()
            return c
        lax.fori_loop(0, MOE_BLOCK, body, 0, unroll=8)

    @pl.when(b == 0)
    def _():
        for k in range(GATHER_AHEAD):
            pl.when(k < nu)(functools.partial(start_rows, tok_first[k], k))

    @pl.when(b + GATHER_AHEAD < nu)
    def _():
        start_rows(tok_ahead, (b + GATHER_AHEAD) % nslot)

    def w_copies(e, s):
        out = []
        for m, (src, dst) in enumerate(((wg_hbm, wgbuf), (wu_hbm, wubuf), (wd_hbm, wdbuf))):
            rows = dst.shape[1] // W_CHUNKS
            for k in range(W_CHUNKS):
                rs = pl.ds(k * rows, rows)
                out.append(pltpu.make_async_copy(src.at[layer, e, rs], dst.at[s, rs], wsem.at[s, m]))
        return out

    def start_weights(e, s):
        for k, c in enumerate(w_copies(e, s)):
            c.start(priority=k % 2)

    @pl.when(b == 0)
    def _():
        start_weights(be_ref[0], 0)

    @pl.when(b < nu)
    def _():
        s = slot_ref[b]

        @pl.when(first_ref[b] == 1)
        def _():
            for c in w_copies(be_ref[b], s):
                c.wait()
            ne = nexte_ref[b]

            @pl.when(ne >= 0)
            def _():
                start_weights(ne, 1 - s)

        def wait_body(r, c):
            row_copy(0, xs, r).wait()
            return c
        lax.fori_loop(0, MOE_BLOCK, wait_body, 0, unroll=8)
        x_u32 = jnp.concatenate([xbuf[xs, :, t, :] for t in range(xbuf.shape[2])], axis=-1)
        x_hi, x_lo = _unpack_bf16_pair(x_u32)
        half = x_hi.shape[1]
        hg = (jnp.dot(x_hi, wgbuf[s, 0:half, :], preferred_element_type=F32)
              + jnp.dot(x_lo, wgbuf[s, half:, :], preferred_element_type=F32))
        hu = (jnp.dot(x_hi, wubuf[s, 0:half, :], preferred_element_type=F32)
              + jnp.dot(x_lo, wubuf[s, half:, :], preferred_element_type=F32))
        hb = (hg * jax.nn.sigmoid(hg) * hu).astype(BF16).astype(F32)
        o_ref[...] = _pack_bf16_pair(jnp.dot(hb, wdbuf[s], preferred_element_type=F32))

    @pl.when(b >= nu)
    def _():
        o_ref[...] = jnp.zeros_like(o_ref)


def moe_experts(hn_slabs, plan, w_gate, w_up, w_down, layer):
    S, RT, _ = hn_slabs.shape
    C = RT * LANES
    D = 2 * C
    nblk = plan["row_tok"].shape[0]
    assert nblk > GATHER_AHEAD
    tok_spec = lambda f: pl.BlockSpec((None, 1, MOE_BLOCK), f, memory_space=pltpu.SMEM)
    fixed = [tok_spec(functools.partial(lambda k, b, *_: (k, 0, 0), k)) for k in range(GATHER_AHEAD)]
    ahead = tok_spec(lambda b, *_: (jnp.minimum(b + GATHER_AHEAD, nblk - 1), 0, 0))
    anyspec = pl.BlockSpec(memory_space=pl.ANY)
    grid_spec = pltpu.PrefetchScalarGridSpec(
        num_scalar_prefetch=5,
        grid=(nblk,),
        in_specs=fixed + [ahead, anyspec, anyspec, anyspec, anyspec],
        out_specs=pl.BlockSpec((MOE_BLOCK, C), lambda b, *_: (b, 0)),
        scratch_shapes=[pltpu.VMEM((GATHER_AHEAD + 1, MOE_BLOCK, RT, LANES), jnp.uint32),
                        pltpu.VMEM((2, D, D_FF), F32),
                        pltpu.VMEM((2, D, D_FF), F32),
                        pltpu.VMEM((2, D_FF, D), F32),
                        pltpu.SemaphoreType.DMA((GATHER_AHEAD + 1,)),
                        pltpu.SemaphoreType.DMA((2, 3))],
    )
    return pl.pallas_call(
        functools.partial(_expert_kernel, layer=layer),
        grid_spec=grid_spec,
        out_shape=jax.ShapeDtypeStruct((nblk * MOE_BLOCK, C), jnp.uint32),
        compiler_params=_params(("arbitrary",), VMEM_LIMIT),
        name="moe_experts",
    )(plan["blk_e"], plan["first"], plan["slot"], plan["next_e"], plan["n_used"],
      *([plan["row_tok"]] * (GATHER_AHEAD + 1)), hn_slabs, w_gate, w_up, w_down)


def _combine_kernel(*refs, tm, nt, emit_h):
    pos_first = refs[:GATHER_AHEAD]
    pos_ahead, h_ref, route_ref, g_ref, y_hbm = refs[GATHER_AHEAD:GATHER_AHEAD + 5]
    rest = refs[GATHER_AHEAD + 5:]
    if emit_h:
        h_out_ref, n_out_ref, ybuf, sem = rest
    else:
        n_out_ref, ybuf, sem = rest
    nslot = GATHER_AHEAD + 1
    i = pl.program_id(0)
    xs = i % nslot

    def row_copy(p, s, j):
        return pltpu.make_async_copy(y_hbm.at[pl.ds(p, 1), :], ybuf.at[s, pl.ds(j, 1), :], sem.at[s])

    def start_rows(pos_ref, s):
        def body(j, c):
            row_copy(pos_ref[0, 2 * j], s, 2 * j).start(priority=0)
            row_copy(pos_ref[0, 2 * j + 1], s, 2 * j + 1).start(priority=1)
            return c
        lax.fori_loop(0, TOP_K * tm // 2, body, 0, unroll=4)

    @pl.when(i == 0)
    def _():
        for k in range(GATHER_AHEAD):
            start_rows(pos_first[k], k)

    @pl.when(i + GATHER_AHEAD < nt)
    def _():
        start_rows(pos_ahead, (i + GATHER_AHEAD) % nslot)

    def wait_body(j, c):
        row_copy(0, xs, j).wait()
        return c
    lax.fori_loop(0, TOP_K * tm, wait_body, 0, unroll=8)

    a_hi, a_lo = _unpack_bf16_pair(ybuf[xs, 0:tm, :])
    b_hi, b_lo = _unpack_bf16_pair(ybuf[xs, tm:2 * tm, :])
    w1 = route_ref[:, TOP_K:TOP_K + 1]
    w2 = route_ref[:, TOP_K + 1:TOP_K + 2]
    half = a_hi.shape[1]
    h_lo = h_ref[:, 0:half] + w1 * a_hi + w2 * b_hi
    h_hi = h_ref[:, half:] + w1 * a_lo + w2 * b_lo
    if emit_h:
        h_out_ref[:, 0:half] = h_lo
        h_out_ref[:, half:] = h_hi
    ms = (jnp.sum(h_lo * h_lo, axis=-1, keepdims=True)
          + jnp.sum(h_hi * h_hi, axis=-1, keepdims=True)) / (2 * half)
    r = lax.rsqrt(ms + EPS)
    n_out_ref[:, 0:half] = (h_lo * r * g_ref[:, 0:half]).astype(n_out_ref.dtype)
    n_out_ref[:, half:] = (h_hi * r * g_ref[:, half:]).astype(n_out_ref.dtype)


def moe_combine(h, route, yb, pos3, g3, g_layer, norm_dtype, emit_h, tm=128):
    S, D = h.shape
    nt = S // tm
    assert nt > GATHER_AHEAD
    pos_spec = lambda f: pl.BlockSpec((None, 1, TOP_K * tm), f, memory_space=pltpu.SMEM)
    out_specs = [pl.BlockSpec((tm, D), lambda i: (i, 0))]
    out_shape = [jax.ShapeDtypeStruct((S, D), norm_dtype)]
    if emit_h:
        out_specs = [pl.BlockSpec((tm, D), lambda i: (i, 0))] + out_specs
        out_shape = [jax.ShapeDtypeStruct((S, D), F32)] + out_shape
    return pl.pallas_call(
        functools.partial(_combine_kernel, tm=tm, nt=nt, emit_h=emit_h),
        grid=(nt,),
        in_specs=[pos_spec(functools.partial(lambda k, i: (k, 0, 0), k)) for k in range(GATHER_AHEAD)] + [
                  pos_spec(lambda i: (jnp.minimum(i + GATHER_AHEAD, nt - 1), 0, 0)),
                  pl.BlockSpec((tm, D), lambda i: (i, 0)),
                  pl.BlockSpec((tm, LANES), lambda i: (i, 0)),
                  pl.BlockSpec((None, 1, D), lambda i: (g_layer, 0, 0)),
                  pl.BlockSpec(memory_space=pl.ANY)],
        out_specs=out_specs,
        out_shape=out_shape,
        scratch_shapes=[pltpu.VMEM((GATHER_AHEAD + 1, TOP_K * tm, D // 2), jnp.uint32),
                        pltpu.SemaphoreType.DMA((GATHER_AHEAD + 1,))],
        compiler_params=_params(("arbitrary",), VMEM_LIMIT),
        name="moe_combine",
    )(*([pos3] * (GATHER_AHEAD + 1)), h, route, g3, yb)


def hierarchical_moe(h, g3, wr_g, br_g, wr_e, br_e, w_gate, w_up, w_down, layer,
                     next_g3, next_layer, next_dtype, emit_h, tm_combine=128):
    S, D = h.shape
    wr = jnp.concatenate([wr_g[layer], jnp.transpose(wr_e[layer], (1, 0, 2)).reshape(D, N_EXPERTS)],
                         axis=1)
    wr = jnp.pad(wr, ((0, 0), (0, LANES - wr.shape[1])))
    wr_hi = wr.astype(BF16)
    wr = jnp.concatenate([wr_hi, (wr - wr_hi.astype(F32)).astype(BF16)], axis=1)
    br =jnp.concatenate([br_g[layer], br_e[layer].reshape(-1)])
    br = jnp.pad(br, (0, LANES - br.shape[0])).reshape(1, LANES)
    hn_slabs, route = moe_router(h, g3, wr, br, layer)
    plan = _moe_plan(route[:, 0:TOP_K].astype(jnp.int32), tm_combine)
    yb = moe_experts(hn_slabs, plan, w_gate, w_up, w_down, layer)
    return moe_combine(h, route, yb, plan["pos3"], next_g3, next_layer, next_dtype, emit_h,
                       tm=tm_combine)


def _moe_plan(expert_id, tm_combine):
    S = expert_id.shape[0]
    A = S * TOP_K
    i32 = jnp.int32
    flat_e = expert_id.reshape(-1)
    order = jnp.argsort(flat_e).astype(i32)
    rank = jnp.argsort(order).astype(i32)
    edges = jnp.searchsorted(flat_e[order], jnp.arange(N_EXPERTS + 1, dtype=i32), side='left').astype(i32)
    start, counts = edges[:-1], edges[1:] - edges[:-1]
    nb_e = (counts + MOE_BLOCK - 1) // MOE_BLOCK
    bend = jnp.cumsum(nb_e)
    pstart = (bend - nb_e) * MOE_BLOCK
    shift = pstart - start
    dshift = shift - jnp.concatenate([jnp.zeros((1,), i32), shift[:-1]])
    pos_flat = rank + jnp.sum(jnp.where(rank[:, None] >= start[None, :], dshift[None, :], 0), axis=1)
    nblk = -(-A // MOE_BLOCK) + N_EXPERTS
    n_used = bend[-1]
    ar = jnp.arange(nblk, dtype=i32)
    blk_e = jnp.minimum(jnp.searchsorted(bend, ar, side='right'), N_EXPERTS - 1).astype(i32)
    base = start[blk_e] + (ar - (bend - nb_e)[blk_e]) * MOE_BLOCK
    sidx = base[:, None] + jnp.arange(MOE_BLOCK, dtype=i32)[None, :]
    live = (sidx < (start + counts)[blk_e][:, None]) & (ar < n_used)[:, None]
    row_tok = jnp.where(live, order[jnp.clip(sidx, 0, A - 1)] // TOP_K, 0)
    first = ((ar == 0) | (blk_e != jnp.roll(blk_e, 1))) & (ar < n_used)
    slot = (jnp.cumsum(first.astype(i32)) - 1) % 2
    big = i32(nblk)
    nf = lax.cummin(jnp.where(first, ar, big)[::-1])[::-1]
    next_first = jnp.concatenate([nf[1:], big[None]])
    next_e = jnp.where(next_first < big, blk_e[jnp.minimum(next_first, nblk - 1)], -1)
    nt = S // tm_combine
    pos3 = pos_flat.reshape(nt, tm_combine, TOP_K).transpose(0, 2, 1).reshape(nt, 1, TOP_K * tm_combine)
    return dict(row_tok=row_tok.reshape(nblk, 1, MOE_BLOCK), pos3=pos3, blk_e=blk_e,
                first=first.astype(i32), slot=slot.astype(i32), next_e=next_e.astype(i32),
                n_used=n_used.astype(i32).reshape(1))


def kernel(x, mem, mix_norm, even_w_in, conv_w, conv_b, conv_norm_g, conv_norm_b, diff_lambda, diff_subln_g, even_w_out, odd_w_in, sgu_norm_g, sgu_norm_b, sgu_w, sgu_b, swa_sinks, odd_w_out, xa_norm, mem_norm, xa_wq, xa_wkv, xa_wo, moe_norm, router_group_w, router_group_b, router_expert_w, router_expert_b, w_gate, w_up, w_down, final_norm):
    B, S, D = x.shape
    assert B == 1
    row3 = lambda p: p.reshape(p.shape[0], 1, p.shape[1])
    h = x.reshape(S, D)
    mem2 = mem.reshape(mem.shape[1], D)
    hn = rmsnorm(h, row3(mix_norm), 0, BF16)
    for layer in range(DEPTH):
        i = layer // 2
        if layer % 2 == 0:
            proj = matmul([hn], even_w_in, i, out_dtype=BF16)
            y_a = conformer_conv(proj, conv_w, row3(conv_b), row3(conv_norm_g), row3(conv_norm_b), i)
            lambda_init = 0.8 - 0.6 * math.exp(-0.3 * layer)
            y_b = diff_attention(proj, diff_lambda, row3(diff_subln_g), i, lambda_init)
            h = matmul([y_a, y_b], even_w_out, i, res=h, out_dtype=F32)
        else:
            proj = matmul([hn], odd_w_in, i, out_dtype=BF16)
            y_c = spatial_gating(proj, row3(sgu_norm_g), row3(sgu_norm_b), sgu_w,
                                 jnp.transpose(sgu_b, (0, 2, 1)), i)
            y_d = sliding_window_attention(proj, swa_sinks, i)
            h = matmul([y_c, y_d], odd_w_out, i, res=h, out_dtype=F32)
        hq = rmsnorm(h, row3(xa_norm), layer, BF16)
        mem_n = rmsnorm(mem2, row3(mem_norm), layer, BF16)
        q = matmul([hq], xa_wq, layer, out_dtype=BF16)
        kv = matmul([mem_n], xa_wkv, layer, out_dtype=BF16)
        o = cross_attention(q, kv)
        h = matmul([o], xa_wo, layer, res=h, out_dtype=F32)
        last = layer == DEPTH - 1
        moe_args = (h, row3(moe_norm), router_group_w, router_group_b, router_expert_w,
                    router_expert_b, w_gate, w_up, w_down, layer)
        if last:
            (out,) = hierarchical_moe(*moe_args, final_norm.reshape(1, 1, D), 0, F32, False)
        else:
            h, hn = hierarchical_moe(*moe_args, row3(mix_norm), layer + 1, BF16, True)
    return out.reshape(B, S, D)
```

```python
import functools
import math

import jax
import jax.numpy as jnp
from jax import lax
from jax.experimental import pallas as pl
from jax.experimental.pallas import tpu as pltpu

F32 = jnp.float32
BF16 = jnp.bfloat16

D_MODEL = 4096
DEPTH = 2
EPS = 1e-5
BLOCK = 128

CONV_CH = 2048
CONV_WIDTH = 31
CONV_GROUPS = 16
CONV_HALO = 32
CONV_SUB = 64

DIFF_HEADS = 8
DIFF_HD = 128
DIFF_QK = DIFF_HEADS * 2 * DIFF_HD
DIFF_V = DIFF_HEADS * 2 * DIFF_HD

SGU_CH = 2048
SGU_GROUPS = 8
SGU_CHUNK = 128

SWA_HEADS = 32
SWA_KV = 4
SWA_HD = 64
WINDOW = 128

XA_HEADS = 4
XA_HD = 256

N_GROUPS = 8
EXP_PER_GROUP = 8
N_EXPERTS = N_GROUPS * EXP_PER_GROUP
TOP_K = 2
D_FF = 384
MOE_BLOCK = 128
GATHER_AHEAD = 3
W_CHUNKS = 8

LANES = 128
SUBLANES = 8
LOG2E = math.log2(math.e)
VMEM_LIMIT = 56 * 1024 * 1024


def _params(sem, vmem=None):
    return pltpu.CompilerParams(dimension_semantics=sem, vmem_limit_bytes=vmem)


def _rmsnorm_kernel(x_ref, g_ref, o_ref):
    x = x_ref[...].astype(F32)
    ms = jnp.mean(x * x, axis=-1, keepdims=True)
    o_ref[...] = (x * lax.rsqrt(ms + EPS) * g_ref[...]).astype(o_ref.dtype)


def rmsnorm(x, g2, layer, out_dtype, tm=256):
    M, D = x.shape
    return pl.pallas_call(
        _rmsnorm_kernel,
        grid=(M // tm,),
        in_specs=[pl.BlockSpec((tm, D), lambda i: (i, 0)),
                  pl.BlockSpec((None, 1, D), lambda i: (layer, 0, 0))],
        out_specs=pl.BlockSpec((tm, D), lambda i: (i, 0)),
        out_shape=jax.ShapeDtypeStruct((M, D), out_dtype),
        compiler_params=_params(("arbitrary",)),
        name="rmsnorm",
    )(x, g2)


def _mm_kernel(*refs, k_splits, has_res):
    n_a = len(k_splits)
    a_refs = refs[:n_a]
    w_ref = refs[n_a]
    r_ref = refs[n_a + 1] if has_res else None
    o_ref = refs[n_a + 1 + has_res]
    wb_ref = refs[n_a + 2 + has_res]

    @pl.when(pl.program_id(1) == 0)
    def _():
        wb_ref[...] = w_ref[...].astype(BF16)

    acc = None
    off = 0
    for a_ref, ks in zip(a_refs, k_splits):
        part = jnp.dot(a_ref[...], wb_ref[off:off + ks, :], preferred_element_type=F32)
        acc = part if acc is None else acc + part
        off += ks
    if has_res:
        acc = acc + r_ref[...]
    o_ref[...] = acc.astype(o_ref.dtype)


MM_WEIGHT_BLOCK_BYTES = 8 * 1024 * 1024
MM_VMEM_BUDGET = 46 * 1024 * 1024


def _mm_tiles(M, K, N, out_bytes, has_res):
    tn = min(N, max(LANES, MM_WEIGHT_BLOCK_BYTES // (K * 4)))
    assert N % tn == 0
    for tm in (1024, 512, 256, 128):
        need = (2 * tm * K * 2 + 2 * K * tn * 4 + K * tn * 2
                + 2 * tm * tn * out_bytes + (2 * tm * tn * 4 if has_res else 0))
        if tm <= M and M % tm == 0 and need <= MM_VMEM_BUDGET:
            return tm, tn
    raise ValueError("no matmul tiling fits")


def matmul(a_list, w3, layer, *, res=None, out_dtype):
    M = a_list[0].shape[0]
    _, K, N = w3.shape
    k_splits = tuple(a.shape[1] for a in a_list)
    assert sum(k_splits) == K
    tm, tn = _mm_tiles(M, K, N, jnp.dtype(out_dtype).itemsize, res is not None)
    in_specs = [pl.BlockSpec((tm, ks), lambda j, i: (i, 0)) for ks in k_splits]
    in_specs.append(pl.BlockSpec((None, K, tn), lambda j, i: (layer, 0, j)))
    args = list(a_list) + [w3]
    if res is not None:
        in_specs.append(pl.BlockSpec((tm, tn), lambda j, i: (i, j)))
        args.append(res)
    return pl.pallas_call(
        functools.partial(_mm_kernel, k_splits=k_splits, has_res=res is not None),
        grid=(N // tn, M // tm),
        in_specs=in_specs,
        out_specs=pl.BlockSpec((tm, tn), lambda j, i: (i, j)),
        out_shape=jax.ShapeDtypeStruct((M, N), out_dtype),
        scratch_shapes=[pltpu.VMEM((K, tn), BF16)],
        compiler_params=_params(("arbitrary", "arbitrary"), VMEM_LIMIT),
        name="matmul",
    )(*args)


def _conv_kernel(a_ref, g_ref, w_ref, b_ref, ng_ref, nb_ref, o_ref, ybuf, stage, shbuf, *, T, LB):
    i = pl.program_id(1)

    @pl.when(i == 0)
    def _():
        ybuf[0:CONV_HALO, :] = jnp.zeros((CONV_HALO, LB), F32)

    @pl.when(i > 0)
    def _():
        ybuf[0:CONV_HALO, :] = ybuf[T:T + CONV_HALO, :]

    a = a_ref[...].astype(F32)
    g = g_ref[...].astype(F32)
    ybuf[CONV_HALO:CONV_HALO + T, :] = a * jax.nn.sigmoid(g)

    first = CONV_HALO - (CONV_WIDTH - 1)
    n_sub = T // CONV_SUB

    def sub_tile(it, carry):
        r0 = pl.multiple_of(it * CONV_SUB, CONV_SUB)
        stage[...] = ybuf[pl.ds(r0, CONV_SUB + CONV_HALO), :]
        for gi in range(LB // LANES):
            cols = slice(gi * LANES, (gi + 1) * LANES)
            acc = jnp.broadcast_to(b_ref[:, cols], (CONV_SUB, LANES))
            for ph in range(SUBLANES):
                taps = range(ph, CONV_WIDTH, SUBLANES)
                n = CONV_SUB + SUBLANES * (len(taps) - 1)
                sh = shbuf.at[gi, ph % 2]
                sh[0:n, :] = stage[first + ph:first + ph + n, cols]
                for a, j in enumerate(taps):
                    acc = acc + w_ref[j:j + 1, cols] * sh[SUBLANES * a:SUBLANES * a + CONV_SUB, :]
            mu = jnp.mean(acc, axis=-1, keepdims=True)
            d = acc - mu
            var = jnp.mean(d * d, axis=-1, keepdims=True)
            y = d * lax.rsqrt(var + EPS) * ng_ref[:, cols] + nb_ref[:, cols]
            o_ref[pl.ds(r0, CONV_SUB), cols] = (y * jax.nn.sigmoid(y)).astype(o_ref.dtype)
        return carry

    lax.fori_loop(0, n_sub, sub_tile, 0)


def conformer_conv(proj, conv_w, conv_b, norm_g, norm_b, li, T=256, LB=512):
    S = proj.shape[0]
    assert CONV_CH // CONV_GROUPS == LANES
    ncb = CONV_CH // LB
    vec = lambda: pl.BlockSpec((None, 1, LB), lambda c, i: (li, 0, c))
    return pl.pallas_call(
        functools.partial(_conv_kernel, T=T, LB=LB),
        grid=(ncb, S // T),
        in_specs=[pl.BlockSpec((T, LB), lambda c, i: (i, c)),
                  pl.BlockSpec((T, LB), lambda c, i: (i, ncb + c)),
                  pl.BlockSpec((None, CONV_WIDTH, LB), lambda c, i: (li, 0, c)),
                  vec(), vec(), vec()],
        out_specs=pl.BlockSpec((T, LB), lambda c, i: (i, c)),
        out_shape=jax.ShapeDtypeStruct((S, CONV_CH), BF16),
        scratch_shapes=[pltpu.VMEM((CONV_HALO + T, LB), F32),
                        pltpu.VMEM((CONV_HALO + CONV_SUB, LB), F32),
                        pltpu.VMEM((LB // LANES, 2, CONV_HALO + CONV_SUB, LANES), F32)],
        compiler_params=_params(("arbitrary", "arbitrary")),
        name="conformer_conv",
    )(proj, proj, conv_w, conv_b, norm_g, norm_b)


def _diffattn_kernel(q_ref, k_ref, v_ref, lam_ref, g_ref, o_ref, acc_ref, m_ref, l_ref, s0_ref, s1_ref,
                     *, tq, lambda_init):
    qi = pl.program_id(1)
    d = DIFF_HD
    tk = 2 * tq
    c_exp = (d ** -0.5) * LOG2E
    nfull = qi // 2
    acc_ref[...] = jnp.zeros_like(acc_ref)
    m_ref[...] = jnp.full_like(m_ref, -jnp.inf)
    l_ref[...] = jnp.zeros_like(l_ref)

    def qk(t, s_ref):
        kblk = k_ref[pl.ds(pl.multiple_of(t * tk, tk), tk), :]
        for c in range(2):
            s_ref[c] = lax.dot_general(q_ref[:, c * d:(c + 1) * d], kblk[:, c * d:(c + 1) * d],
                                       (((1,), (1,)), ((), ())), preferred_element_type=F32)

    def softmax_pv(t, s_ref, width, masked, row0):
        vblk = v_ref[pl.ds(pl.multiple_of(t * tk, tk), width), :]
        for c in range(2):
            s = s_ref[c, :, 0:width]
            if masked:
                row = lax.broadcasted_iota(jnp.int32, (tq, width), 0) + row0
                col = lax.broadcasted_iota(jnp.int32, (tq, width), 1)
                s = jnp.where(row >= col, s, -jnp.inf)
            m = m_ref[c]
            m_new = jnp.maximum(m, jnp.max(s, axis=-1, keepdims=True))
            alpha = jnp.exp2((m - m_new) * c_exp)
            p = jnp.exp2((s - m_new) * c_exp)
            l_ref[c] = alpha * l_ref[c] + jnp.sum(p, axis=-1, keepdims=True)
            m_ref[c] = m_new
            acc_ref[c] = alpha * acc_ref[c] + jnp.dot(p.astype(BF16), vblk,
                                                      preferred_element_type=F32)

    @pl.when(nfull % 2 == 0)
    def _():
        qk(0, s0_ref)

    @pl.when(nfull % 2 == 1)
    def _():
        qk(0, s1_ref)

    def body(t, carry):
        par = (nfull - t) % 2

        @pl.when(par == 0)
        def _():
            qk(t + 1, s1_ref)
            softmax_pv(t, s0_ref, tk, False, 0)

        @pl.when(par == 1)
        def _():
            qk(t + 1, s0_ref)
            softmax_pv(t, s1_ref, tk, False, 0)
        return carry

    lax.fori_loop(0, nfull, body, 0)

    @pl.when(qi % 2 == 0)
    def _():
        softmax_pv(nfull, s0_ref, tq, True, 0)

    @pl.when(qi % 2 == 1)
    def _():
        softmax_pv(nfull, s0_ref, tk, True, tq)

    lv = lam_ref[...]
    lam = (jnp.exp(jnp.sum(lv[0:1] * lv[1:2], axis=-1, keepdims=True))
           - jnp.exp(jnp.sum(lv[2:3] * lv[3:4], axis=-1, keepdims=True)) + lambda_init)
    o = acc_ref[0] / l_ref[0] - lam * (acc_ref[1] / l_ref[1])
    ms = jnp.mean(o * o, axis=-1, keepdims=True)
    o = o * lax.rsqrt(ms + EPS) * g_ref[...] * (1.0 - lambda_init)
    o_ref[...] = o.astype(o_ref.dtype)


def diff_attention(proj, lam_vecs, subln_g3, li, lambda_init, tq=512):
    S = proj.shape[0]
    assert S % (2 * tq) == 0
    hw = 2 * DIFF_HD
    q0 = 2 * CONV_CH // hw
    k0 = q0 + DIFF_QK // hw
    v0 = k0 + DIFF_QK // hw
    return pl.pallas_call(
        functools.partial(_diffattn_kernel, tq=tq, lambda_init=lambda_init),
        grid=(DIFF_HEADS, S // tq),
        in_specs=[pl.BlockSpec((tq, hw), lambda h, i: (i, q0 + h)),
                  pl.BlockSpec((S, hw), lambda h, i: (0, k0 + h)),
                  pl.BlockSpec((S, hw), lambda h, i: (0, v0 + h)),
                  pl.BlockSpec((None, 4, DIFF_HD), lambda h, i: (li, 0, 0)),
                  pl.BlockSpec((None, 1, hw), lambda h, i: (li, 0, 0))],
        out_specs=pl.BlockSpec((tq, hw), lambda h, i: (i, h)),
        out_shape=jax.ShapeDtypeStruct((S, DIFF_V), BF16),
        scratch_shapes=[pltpu.VMEM((2, tq, hw), F32),
                        pltpu.VMEM((2, tq, 1), F32),
                        pltpu.VMEM((2, tq, 1), F32),
                        pltpu.VMEM((2, tq, 2 * tq), F32),
                        pltpu.VMEM((2, tq, 2 * tq), F32)],
        compiler_params=_params(("arbitrary", "arbitrary"), VMEM_LIMIT),
        name="diff_attention",
    )(proj, proj, proj, lam_vecs, subln_g3)


def _sgu_kernel(u_ref, v_ref, ng_ref, nb_ref, w_ref, bs_ref, o_ref):
    T = SGU_CHUNK
    gw = SGU_CH // SGU_GROUPS
    row = lax.broadcasted_iota(jnp.int32, (T, T), 0)
    col = lax.broadcasted_iota(jnp.int32, (T, T), 1)
    causal = row >= col
    for g in range(SGU_GROUPS):
        cols = slice(g * gw, (g + 1) * gw)
        u = jax.nn.gelu(u_ref[:, cols].astype(F32), approximate=True)
        v = jax.nn.gelu(v_ref[:, cols].astype(F32), approximate=True)
        mu = jnp.mean(v, axis=-1, keepdims=True)
        d = v - mu
        var = jnp.mean(d * d, axis=-1, keepdims=True)
        vn = d * lax.rsqrt(var + EPS) * ng_ref[:, cols] + nb_ref[:, cols]
        w = jnp.where(causal, w_ref[g], 0.0).astype(BF16)
        mixed = jnp.dot(w, vn.astype(BF16), preferred_element_type=F32) + bs_ref[:, g:g + 1]
        o_ref[:, cols] = (u * mixed).astype(o_ref.dtype)


def spatial_gating(proj, norm_g3, norm_b3, w_s, b_s_t, li):
    S = proj.shape[0]
    T = SGU_CHUNK
    vec = lambda: pl.BlockSpec((None, 1, SGU_CH), lambda i: (li, 0, 0))
    return pl.pallas_call(
        _sgu_kernel,
        grid=(S // T,),
        in_specs=[pl.BlockSpec((T, SGU_CH), lambda i: (i, 0)),
                  pl.BlockSpec((T, SGU_CH), lambda i: (i, 1)),
                  vec(), vec(),
                  pl.BlockSpec((None, SGU_GROUPS, T, T), lambda i: (li, 0, 0, 0)),
                  pl.BlockSpec((None, T, SGU_GROUPS), lambda i: (li, 0, 0))],
        out_specs=pl.BlockSpec((T, SGU_CH), lambda i: (i, 0)),
        out_shape=jax.ShapeDtypeStruct((S, SGU_CH), BF16),
        compiler_params=_params(("arbitrary",)),
        name="spatial_gating",
    )(proj, proj, norm_g3, norm_b3, w_s, b_s_t)


def _swa_kernel(sink_ref, q_ref, kp_ref, kc_ref, vp_ref, vc_ref, o_ref, *, li):
    n = pl.program_id(0)
    T = BLOCK
    G = SWA_HEADS // SWA_KV
    d = SWA_HD
    kk = jnp.concatenate([kp_ref[...], kc_ref[...]], axis=0)
    vv = jnp.concatenate([vp_ref[...], vc_ref[...]], axis=0)
    qi = lax.broadcasted_iota(jnp.int32, (T, 2 * T), 0)
    kj = lax.broadcasted_iota(jnp.int32, (T, 2 * T), 1)
    rel = qi + T - kj
    valid = (rel >= 0) & (rel < WINDOW) & ((n * T + kj - T) >= 0)
    scale = d ** -0.5
    for h in range(SWA_KV):
        kh = kk[:, h * d:(h + 1) * d]
        vh = vv[:, h * d:(h + 1) * d]
        outs = []
        for g in range(G):
            c0 = (h * G + g) * d
            s = lax.dot_general(q_ref[:, c0:c0 + d], kh, (((1,), (1,)), ((), ())),
                                preferred_element_type=F32) * scale
            s = jnp.where(valid, s, -jnp.inf)
            sink = sink_ref[li, h * G + g]
            m = jnp.maximum(jnp.max(s, axis=-1, keepdims=True), sink)
            p = jnp.exp(s - m)
            denom = jnp.sum(p, axis=-1, keepdims=True) + jnp.exp(sink - m)
            outs.append(jnp.dot(p.astype(BF16), vh, preferred_element_type=F32) / denom)
        o_ref[:, h * G * d:(h + 1) * G * d] = jnp.concatenate(outs, axis=-1).astype(o_ref.dtype)


def sliding_window_attention(proj, sinks, li):
    S = proj.shape[0]
    T = BLOCK
    qw = SWA_HEADS * SWA_HD
    kw = SWA_KV * SWA_HD
    qb = 2 * SGU_CH // qw
    kb = (2 * SGU_CH + qw) // kw
    vb = kb + 1
    prev = lambda i: jnp.maximum(i - 1, 0)
    return pl.pallas_call(
        functools.partial(_swa_kernel, li=li),
        grid=(S // T,),
        in_specs=[pl.BlockSpec(memory_space=pltpu.SMEM),
                  pl.BlockSpec((T, qw), lambda i: (i, qb)),
                  pl.BlockSpec((T, kw), lambda i: (prev(i), kb)),
                  pl.BlockSpec((T, kw), lambda i: (i, kb)),
                  pl.BlockSpec((T, kw), lambda i: (prev(i), vb)),
                  pl.BlockSpec((T, kw), lambda i: (i, vb))],
        out_specs=pl.BlockSpec((T, qw), lambda i: (i, 0)),
        out_shape=jax.ShapeDtypeStruct((S, qw), BF16),
        compiler_params=_params(("arbitrary",)),
        name="sliding_window_attention",
    )(sinks, proj, proj, proj, proj, proj)


def _xattn_kernel(q_ref, kv_ref, o_ref):
    d = XA_HD
    c_exp = (d ** -0.5) * LOG2E
    for h in range(XA_HEADS):
        q = q_ref[:, h * d:(h + 1) * d]
        k = kv_ref[:, h * d:(h + 1) * d]
        v = kv_ref[:, (XA_HEADS + h) * d:(XA_HEADS + h + 1) * d]
        s = lax.dot_general(q, k, (((1,), (1,)), ((), ())), preferred_element_type=F32)
        m = jnp.max(s, axis=-1, keepdims=True)
        p = jnp.exp2((s - m) * c_exp)
        l = jnp.sum(p, axis=-1, keepdims=True)
        o = jnp.dot(p.astype(BF16), v, preferred_element_type=F32) / l
        o_ref[:, h * d:(h + 1) * d] = o.astype(o_ref.dtype)


def cross_attention(q, kv, tq=512):
    S, qw = q.shape
    M = kv.shape[0]
    return pl.pallas_call(
        _xattn_kernel,
        grid=(S // tq,),
        in_specs=[pl.BlockSpec((tq, qw), lambda i: (i, 0)),
                  pl.BlockSpec((M, 2 * qw), lambda i: (0, 0))],
        out_specs=pl.BlockSpec((tq, qw), lambda i: (i, 0)),
        out_shape=jax.ShapeDtypeStruct((S, qw), BF16),
        compiler_params=_params(("arbitrary",)),
        name="cross_attention",
    )(q, kv)


def _pack_bf16_pair(y):
    c = y.shape[1] // 2
    hi = lax.bitcast_convert_type(y[:, :c].astype(BF16).astype(F32), jnp.uint32)
    lo = lax.bitcast_convert_type(y[:, c:].astype(BF16).astype(F32), jnp.uint32)
    return hi | lax.shift_right_logical(lo, jnp.uint32(16))


def _unpack_bf16_pair(u):
    hi = lax.bitcast_convert_type(u & jnp.uint32(0xFFFF0000), F32)
    lo = lax.bitcast_convert_type(lax.shift_left(u, jnp.uint32(16)), F32)
    return hi, lo


def _router_kernel(h_ref, g_ref, wr_ref, br_ref, hn_ref, route_ref):
    x = h_ref[...]
    ms = jnp.mean(x * x, axis=-1, keepdims=True)
    hn = x * lax.rsqrt(ms + EPS) * g_ref[...]
    packed = _pack_bf16_pair(hn)
    for t in range(hn_ref.shape[1]):
        hn_ref[:, t, :] = packed[:, t * LANES:(t + 1) * LANES]
    hn_hi = hn.astype(BF16)
    hn_lo = (hn - hn_hi.astype(F32)).astype(BF16)
    z = jnp.dot(hn_hi, wr_ref[...], preferred_element_type=F32)
    logits = (z[:, :LANES] + z[:, LANES:]
              + jnp.dot(hn_lo, wr_ref[:, :LANES], preferred_element_type=F32) + br_ref[...])
    tm = x.shape[0]
    lane = lax.broadcasted_iota(jnp.int32, (tm, LANES), 1)
    big = jnp.int32(2 * LANES)
    gl = jnp.where(lane < N_GROUPS, logits, -jnp.inf)
    gmax = jnp.max(gl, axis=-1, keepdims=True)
    gsel = jnp.min(jnp.where(gl == gmax, lane, big), axis=-1, keepdims=True)
    g_w = 1.0 / jnp.sum(jnp.exp(gl - gmax), axis=-1, keepdims=True)
    lo = N_GROUPS + gsel * EXP_PER_GROUP
    el = jnp.where((lane >= lo) & (lane < lo + EXP_PER_GROUP), logits, -jnp.inf)
    e1 = jnp.max(el, axis=-1, keepdims=True)
    i1 = jnp.min(jnp.where(el == e1, lane, big), axis=-1, keepdims=True)
    el2 = jnp.where(lane == i1, -jnp.inf, el)
    e2 = jnp.max(el2, axis=-1, keepdims=True)
    i2 = jnp.min(jnp.where(el2 == e2, lane, big), axis=-1, keepdims=True)
    t = jnp.exp(e2 - e1)
    w1 = g_w / (1.0 + t)
    w2 = g_w * t / (1.0 + t)
    id1 = (i1 - N_GROUPS).astype(F32)
    id2 = (i2 - N_GROUPS).astype(F32)
    route = jnp.where(lane == 0, id1, jnp.where(lane == 1, id2,
                      jnp.where(lane == 2, w1, jnp.where(lane == 3, w2, 0.0))))
    route_ref[...] = route


def moe_router(h, g3, wr, br, layer, tm=256):
    S, D = h.shape
    return pl.pallas_call(
        _router_kernel,
        grid=(S // tm,),
        in_specs=[pl.BlockSpec((tm, D), lambda i: (i, 0)),
                  pl.BlockSpec((None, 1, D), lambda i: (layer, 0, 0)),
                  pl.BlockSpec((D, 2 * LANES), lambda i: (0, 0)),
                  pl.BlockSpec((1, LANES), lambda i: (0, 0))],
        out_specs=[pl.BlockSpec((tm, D // 2 // LANES, LANES), lambda i: (i, 0, 0)),
                   pl.BlockSpec((tm, LANES), lambda i: (i, 0))],
        out_shape=[jax.ShapeDtypeStruct((S, D // 2 // LANES, LANES), jnp.uint32),
                   jax.ShapeDtypeStruct((S, LANES), F32)],
        compiler_params=_params(("arbitrary",), VMEM_LIMIT),
        name="moe_router",
    )(h, g3, wr, br)


def _expert_kernel(be_ref, first_ref, slot_ref, nexte_ref, nu_ref, *refs, layer):
    tok_first = refs[:GATHER_AHEAD]
    (tok_ahead, hn_hbm, wg_hbm, wu_hbm, wd_hbm,
     o_ref, xbuf, wgbuf, wubuf, wdbuf, xsem, wsem) = refs[GATHER_AHEAD:]
    nslot = GATHER_AHEAD + 1
    b = pl.program_id(0)
    nu = nu_ref[0]
    xs = b % nslot

    def row_copy(t, s, r):
        return pltpu.make_async_copy(hn_hbm.at[pl.ds(t, 1)], xbuf.at[s, pl.ds(r, 1)], xsem.at[s])

    def start_rows(tok_ref, s):
        def body(j, c):
            row_copy(tok_ref[0, 2 * j], s, 2 * j).start(priority=0)
            row_copy(tok_ref[0, 2 * j + 1], s, 2 * j + 1).start(priority=1)
            return c
        lax.fori_loop(0, MOE_BLOCK // 2, body, 0, unroll=4)

    @pl.when(b == 0)
    def _():
        for k in range(GATHER_AHEAD):
            pl.when(k < nu)(functools.partial(start_rows, tok_first[k], k))

    @pl.when(b + GATHER_AHEAD < nu)
    def _():
        start_rows(tok_ahead, (b + GATHER_AHEAD) % nslot)

    def w_copies(e, s):
        out = []
        for m, (src, dst) in enumerate(((wg_hbm, wgbuf), (wu_hbm, wubuf), (wd_hbm, wdbuf))):
            rows = dst.shape[1] // W_CHUNKS
            for k in range(W_CHUNKS):
                rs = pl.ds(k * rows, rows)
                out.append(pltpu.make_async_copy(src.at[layer, e, rs], dst.at[s, rs], wsem.at[s, m]))
        return out

    def start_weights(e, s):
        for k, c in enumerate(w_copies(e, s)):
            c.start(priority=k % 2)

    @pl.when(b == 0)
    def _():
        start_weights(be_ref[0], 0)

    @pl.when(b < nu)
    def _():
        s = slot_ref[b]

        @pl.when(first_ref[b] == 1)
        def _():
            for c in w_copies(be_ref[b], s):
                c.wait()
            ne = nexte_ref[b]

            @pl.when(ne >= 0)
            def _():
                start_weights(ne, 1 - s)

        def wait_body(r, c):
            row_copy(0, xs, r).wait()
            return c
        lax.fori_loop(0, MOE_BLOCK, wait_body, 0, unroll=8)
        x_u32 = jnp.concatenate([xbuf[xs, :, t, :] for t in range(xbuf.shape[2])], axis=-1)
        x_hi, x_lo = _unpack_bf16_pair(x_u32)
        half = x_hi.shape[1]
        hg = (jnp.dot(x_hi, wgbuf[s, 0:half, :], preferred_element_type=F32)
              + jnp.dot(x_lo, wgbuf[s, half:, :], preferred_element_type=F32))
        hu = (jnp.dot(x_hi, wubuf[s, 0:half, :], preferred_element_type=F32)
              + jnp.dot(x_lo, wubuf[s, half:, :], preferred_element_type=F32))
        hb = (hg * jax.nn.sigmoid(hg) * hu).astype(BF16).astype(F32)
        o_ref[...] = _pack_bf16_pair(jnp.dot(hb, wdbuf[s], preferred_element_type=F32))

    @pl.when(b >= nu)
    def _():
        o_ref[...] = jnp.zeros_like(o_ref)


def moe_experts(hn_slabs, plan, w_gate, w_up, w_down, layer):
    S, RT, _ = hn_slabs.shape
    C = RT * LANES
    D = 2 * C
    nblk = plan["row_tok"].shape[0]
    assert nblk > GATHER_AHEAD
    tok_spec = lambda f: pl.BlockSpec((None, 1, MOE_BLOCK), f, memory_space=pltpu.SMEM)
    fixed = [tok_spec(functools.partial(lambda k, b, *_: (k, 0, 0), k)) for k in range(GATHER_AHEAD)]
    ahead = tok_spec(lambda b, *_: (jnp.minimum(b + GATHER_AHEAD, nblk - 1), 0, 0))
    anyspec = pl.BlockSpec(memory_space=pl.ANY)
    grid_spec = pltpu.PrefetchScalarGridSpec(
        num_scalar_prefetch=5,
        grid=(nblk,),
        in_specs=fixed + [ahead, anyspec, anyspec, anyspec, anyspec],
        out_specs=pl.BlockSpec((MOE_BLOCK, C), lambda b, *_: (b, 0)),
        scratch_shapes=[pltpu.VMEM((GATHER_AHEAD + 1, MOE_BLOCK, RT, LANES), jnp.uint32),
                        pltpu.VMEM((2, D, D_FF), F32),
                        pltpu.VMEM((2, D, D_FF), F32),
                        pltpu.VMEM((2, D_FF, D), F32),
                        pltpu.SemaphoreType.DMA((GATHER_AHEAD + 1,)),
                        pltpu.SemaphoreType.DMA((2, 3))],
    )
    return pl.pallas_call(
        functools.partial(_expert_kernel, layer=layer),
        grid_spec=grid_spec,
        out_shape=jax.ShapeDtypeStruct((nblk * MOE_BLOCK, C), jnp.uint32),
        compiler_params=_params(("arbitrary",), VMEM_LIMIT),
        name="moe_experts",
    )(plan["blk_e"], plan["first"], plan["slot"], plan["next_e"], plan["n_used"],
      *([plan["row_tok"]] * (GATHER_AHEAD + 1)), hn_slabs, w_gate, w_up, w_down)


def _combine_kernel(*refs, tm, nt, emit_h):
    pos_first = refs[:GATHER_AHEAD]
    pos_ahead, h_ref, route_ref, g_ref, y_hbm = refs[GATHER_AHEAD:GATHER_AHEAD + 5]
    rest = refs[GATHER_AHEAD + 5:]
    if emit_h:
        h_out_ref, n_out_ref, ybuf, sem = rest
    else:
        n_out_ref, ybuf, sem = rest
    nslot = GATHER_AHEAD + 1
    i = pl.program_id(0)
    xs = i % nslot

    def row_copy(p, s, j):
        return pltpu.make_async_copy(y_hbm.at[pl.ds(p, 1), :], ybuf.at[s, pl.ds(j, 1), :], sem.at[s])

    def start_rows(pos_ref, s):
        def body(j, c):
            row_copy(pos_ref[0, 2 * j], s, 2 * j).start(priority=0)
            row_copy(pos_ref[0, 2 * j + 1], s, 2 * j + 1).start(priority=1)
            return c
        lax.fori_loop(0, TOP_K * tm // 2, body, 0, unroll=4)

    @pl.when(i == 0)
    def _():
        for k in range(GATHER_AHEAD):
            start_rows(pos_first[k], k)

    @pl.when(i + GATHER_AHEAD < nt)
    def _():
        start_rows(pos_ahead, (i + GATHER_AHEAD) % nslot)

    def wait_body(j, c):
        row_copy(0, xs, j).wait()
        return c
    lax.fori_loop(0, TOP_K * tm, wait_body, 0, unroll=8)

    a_hi, a_lo = _unpack_bf16_pair(ybuf[xs, 0:tm, :])
    b_hi, b_lo = _unpack_bf16_pair(ybuf[xs, tm:2 * tm, :])
    w1 = route_ref[:, TOP_K:TOP_K + 1]
    w2 = route_ref[:, TOP_K + 1:TOP_K + 2]
    half = a_hi.shape[1]
    h_lo = h_ref[:, 0:half] + w1 * a_hi + w2 * b_hi
    h_hi = h_ref[:, half:] + w1 * a_lo + w2 * b_lo
    if emit_h:
        h_out_ref[:, 0:half] = h_lo
        h_out_ref[:, half:] = h_hi
    ms = (jnp.sum(h_lo * h_lo, axis=-1, keepdims=True)
          + jnp.sum(h_hi * h_hi, axis=-1, keepdims=True)) / (2 * half)
    r = lax.rsqrt(ms + EPS)
    n_out_ref[:, 0:half] = (h_lo * r * g_ref[:, 0:half]).astype(n_out_ref.dtype)
    n_out_ref[:, half:] = (h_hi * r * g_ref[:, half:]).astype(n_out_ref.dtype)


def moe_combine(h, route, yb, pos3, g3, g_layer, norm_dtype, emit_h, tm=128):
    S, D = h.shape
    nt = S // tm
    assert nt > GATHER_AHEAD
    pos_spec = lambda f: pl.BlockSpec((None, 1, TOP_K * tm), f, memory_space=pltpu.SMEM)
    out_specs = [pl.BlockSpec((tm, D), lambda i: (i, 0))]
    out_shape = [jax.ShapeDtypeStruct((S, D), norm_dtype)]
    if emit_h:
        out_specs = [pl.BlockSpec((tm, D), lambda i: (i, 0))] + out_specs
        out_shape = [jax.ShapeDtypeStruct((S, D), F32)] + out_shape
    return pl.pallas_call(
        functools.partial(_combine_kernel, tm=tm, nt=nt, emit_h=emit_h),
        grid=(nt,),
        in_specs=[pos_spec(functools.partial(lambda k, i: (k, 0, 0), k)) for k in range(GATHER_AHEAD)] + [
                  pos_spec(lambda i: (jnp.minimum(i + GATHER_AHEAD, nt - 1), 0, 0)),
                  pl.BlockSpec((tm, D), lambda i: (i, 0)),
                  pl.BlockSpec((tm, LANES), lambda i: (i, 0)),
                  pl.BlockSpec((None, 1, D), lambda i: (g_layer, 0, 0)),
                  pl.BlockSpec(memory_space=pl.ANY)],
        out_specs=out_specs,
        out_shape=out_shape,
        scratch_shapes=[pltpu.VMEM((GATHER_AHEAD + 1, TOP_K * tm, D // 2), jnp.uint32),
                        pltpu.SemaphoreType.DMA((GATHER_AHEAD + 1,))],
        compiler_params=_params(("arbitrary",), VMEM_LIMIT),
        name="moe_combine",
    )(*([pos3] * (GATHER_AHEAD + 1)), h, route, g3, yb)


def hierarchical_moe(h, g3, wr_g, br_g, wr_e, br_e, w_gate, w_up, w_down, layer,
                     next_g3, next_layer, next_dtype, emit_h, tm_combine=128):
    S, D = h.shape
    wr = jnp.concatenate([wr_g[layer], jnp.transpose(wr_e[layer], (1, 0, 2)).reshape(D, N_EXPERTS)],
                         axis=1)
    wr = jnp.pad(wr, ((0, 0), (0, LANES - wr.shape[1])))
    wr_hi = wr.astype(BF16)
    wr = jnp.concatenate([wr_hi, (wr - wr_hi.astype(F32)).astype(BF16)], axis=1)
    br =jnp.concatenate([br_g[layer], br_e[layer].reshape(-1)])
    br = jnp.pad(br, (0, LANES - br.shape[0])).reshape(1, LANES)
    hn_slabs, route = moe_router(h, g3, wr, br, layer)
    plan = _moe_plan(route[:, 0:TOP_K].astype(jnp.int32), tm_combine)
    yb = moe_experts(hn_slabs, plan, w_gate, w_up, w_down, layer)
    return moe_combine(h, route, yb, plan["pos3"], next_g3, next_layer, next_dtype, emit_h,
                       tm=tm_combine)


def _moe_plan(expert_id, tm_combine):
    S = expert_id.shape[0]
    A = S * TOP_K
    i32 = jnp.int32
    flat_e = expert_id.reshape(-1)
    order = jnp.argsort(flat_e).astype(i32)
    rank = jnp.argsort(order).astype(i32)
    edges = jnp.searchsorted(flat_e[order], jnp.arange(N_EXPERTS + 1, dtype=i32), side='left').astype(i32)
    start, counts = edges[:-1], edges[1:] - edges[:-1]
    nb_e = (counts + MOE_BLOCK - 1) // MOE_BLOCK
    bend = jnp.cumsum(nb_e)
    pstart = (bend - nb_e) * MOE_BLOCK
    shift = pstart - start
    dshift = shift - jnp.concatenate([jnp.zeros((1,), i32), shift[:-1]])
    pos_flat = rank + jnp.sum(jnp.where(rank[:, None] >= start[None, :], dshift[None, :], 0), axis=1)
    nblk = -(-A // MOE_BLOCK) + N_EXPERTS
    n_used = bend[-1]
    ar = jnp.arange(nblk, dtype=i32)
    blk_e = jnp.minimum(jnp.searchsorted(bend, ar, side='right'), N_EXPERTS - 1).astype(i32)
    base = start[blk_e] + (ar - (bend - nb_e)[blk_e]) * MOE_BLOCK
    sidx = base[:, None] + jnp.arange(MOE_BLOCK, dtype=i32)[None, :]
    live = (sidx < (start + counts)[blk_e][:, None]) & (ar < n_used)[:, None]
    row_tok = jnp.where(live, order[jnp.clip(sidx, 0, A - 1)] // TOP_K, 0)
    first = ((ar == 0) | (blk_e != jnp.roll(blk_e, 1))) & (ar < n_used)
    slot = (jnp.cumsum(first.astype(i32)) - 1) % 2
    big = i32(nblk)
    nf = lax.cummin(jnp.where(first, ar, big)[::-1])[::-1]
    next_first = jnp.concatenate([nf[1:], big[None]])
    next_e = jnp.where(next_first < big, blk_e[jnp.minimum(next_first, nblk - 1)], -1)
    nt = S // tm_combine
    pos3 = pos_flat.reshape(nt, tm_combine, TOP_K).transpose(0, 2, 1).reshape(nt, 1, TOP_K * tm_combine)
    return dict(row_tok=row_tok.reshape(nblk, 1, MOE_BLOCK), pos3=pos3, blk_e=blk_e,
                first=first.astype(i32), slot=slot.astype(i32), next_e=next_e.astype(i32),
                n_used=n_used.astype(i32).reshape(1))


def kernel(x, mem, mix_norm, even_w_in, conv_w, conv_b, conv_norm_g, conv_norm_b, diff_lambda, diff_subln_g, even_w_out, odd_w_in, sgu_norm_g, sgu_norm_b, sgu_w, sgu_b, swa_sinks, odd_w_out, xa_norm, mem_norm, xa_wq, xa_wkv, xa_wo, moe_norm, router_group_w, router_group_b, router_expert_w, router_expert_b, w_gate, w_up, w_down, final_norm):
    B, S, D = x.shape
    assert B == 1
    row3 = lambda p: p.reshape(p.shape[0], 1, p.shape[1])
    h = x.reshape(S, D)
    mem2 = mem.reshape(mem.shape[1], D)
    hn = rmsnorm(h, row3(mix_norm), 0, BF16)
    for layer in range(DEPTH):
        i = layer // 2
        if layer % 2 == 0:
            proj = matmul([hn], even_w_in, i, out_dtype=BF16)
            y_a = conformer_conv(proj, conv_w, row3(conv_b), row3(conv_norm_g), row3(conv_norm_b), i)
            lambda_init = 0.8 - 0.6 * math.exp(-0.3 * layer)
            y_b = diff_attention(proj, diff_lambda, row3(diff_subln_g), i, lambda_init)
            h = matmul([y_a, y_b], even_w_out, i, res=h, out_dtype=F32)
        else:
            proj = matmul([hn], odd_w_in, i, out_dtype=BF16)
            y_c = spatial_gating(proj, row3(sgu_norm_g), row3(sgu_norm_b), sgu_w,
                                 jnp.transpose(sgu_b, (0, 2, 1)), i)
            y_d = sliding_window_attention(proj, swa_sinks, i)
            h = matmul([y_c, y_d], odd_w_out, i, res=h, out_dtype=F32)
        hq = rmsnorm(h, row3(xa_norm), layer, BF16)
        mem_n = rmsnorm(mem2, row3(mem_norm), layer, BF16)
        q = matmul([hq], xa_wq, layer, out_dtype=BF16)
        kv = matmul([mem_n], xa_wkv, layer, out_dtype=BF16)
        o = cross_attention(q, kv)
        h = matmul([o], xa_wo, layer, res=h, out_dtype=F32)
        last = layer == DEPTH - 1
        moe_args = (h, row3(moe_norm), router_group_w, router_group_b, router_expert_w,
                    router_expert_b, w_gate, w_up, w_down, layer)
        if last:
            (out,) = hierarchical_moe(*moe_args, final_norm.reshape(1, 1, D), 0, F32, False)
        else:
            h, hn = hierarchical_moe(*moe_args, row3(mix_norm), layer + 1, BF16, True)
    return out.reshape(B, S, D)
```

```python
import functools
import math

import jax
import jax.numpy as jnp
from jax import lax
from jax.experimental import pallas as pl
from jax.experimental.pallas import tpu as pltpu

F32 = jnp.float32
BF16 = jnp.bfloat16

D_MODEL = 4096
DEPTH = 2
EPS = 1e-5
BLOCK = 128

CONV_CH = 2048
CONV_WIDTH = 31
CONV_GROUPS = 16
CONV_HALO = 32
CONV_SUB = 64

DIFF_HEADS = 8
DIFF_HD = 128
DIFF_QK = DIFF_HEADS * 2 * DIFF_HD
DIFF_V = DIFF_HEADS * 2 * DIFF_HD

SGU_CH = 2048
SGU_GROUPS = 8
SGU_CHUNK = 128

SWA_HEADS = 32
SWA_KV = 4
SWA_HD = 64
WINDOW = 128

XA_HEADS = 4
XA_HD = 256

N_GROUPS = 8
EXP_PER_GROUP = 8
N_EXPERTS = N_GROUPS * EXP_PER_GROUP
TOP_K = 2
D_FF = 384
MOE_BLOCK = 128
GATHER_AHEAD = 3
W_CHUNKS = 8

LANES = 128
SUBLANES = 8
LOG2E = math.log2(math.e)
VMEM_LIMIT = 56 * 1024 * 1024


def _params(sem, vmem=None):
    return pltpu.CompilerParams(dimension_semantics=sem, vmem_limit_bytes=vmem)


def _rmsnorm_kernel(x_ref, g_ref, o_ref):
    x = x_ref[...].astype(F32)
    ms = jnp.mean(x * x, axis=-1, keepdims=True)
    o_ref[...] = (x * lax.rsqrt(ms + EPS) * g_ref[...]).astype(o_ref.dtype)


def rmsnorm(x, g2, layer, out_dtype, tm=512):
    M, D = x.shape
    tm = min(tm, M)
    return pl.pallas_call(
        _rmsnorm_kernel,
        grid=(M // tm,),
        in_specs=[pl.BlockSpec((tm, D), lambda i: (i, 0)),
                  pl.BlockSpec((None, 1, D), lambda i: (layer, 0, 0))],
        out_specs=pl.BlockSpec((tm, D), lambda i: (i, 0)),
        out_shape=jax.ShapeDtypeStruct((M, D), out_dtype),
        compiler_params=_params(("arbitrary",)),
        name="rmsnorm",
    )(x, g2)


def _mm_kernel(*refs, k_splits, has_res):
    n_a = len(k_splits)
    a_refs = refs[:n_a]
    w_ref = refs[n_a]
    r_ref = refs[n_a + 1] if has_res else None
    o_ref = refs[n_a + 1 + has_res]
    wb_ref = refs[n_a + 2 + has_res]

    @pl.when(pl.program_id(1) == 0)
    def _():
        wb_ref[...] = w_ref[...].astype(BF16)

    acc = None
    off = 0
    for a_ref, ks in zip(a_refs, k_splits):
        part = jnp.dot(a_ref[...], wb_ref[off:off + ks, :], preferred_element_type=F32)
        acc = part if acc is None else acc + part
        off += ks
    if has_res:
        acc = acc + r_ref[...]
    o_ref[...] = acc.astype(o_ref.dtype)


MM_WEIGHT_BLOCK_BYTES = 8 * 1024 * 1024
MM_VMEM_BUDGET = 46 * 1024 * 1024


def _mm_tiles(M, K, N, out_bytes, has_res):
    tn = min(N, max(LANES, MM_WEIGHT_BLOCK_BYTES // (K * 4)))
    assert N % tn == 0
    for tm in (1024, 512, 256, 128):
        need = (2 * tm * K * 2 + 2 * K * tn * 4 + K * tn * 2
                + 2 * tm * tn * out_bytes + (2 * tm * tn * 4 if has_res else 0))
        if tm <= M and M % tm == 0 and need <= MM_VMEM_BUDGET:
            return tm, tn
    raise ValueError("no matmul tiling fits")


def matmul(a_list, w3, layer, *, res=None, out_dtype):
    M = a_list[0].shape[0]
    _, K, N = w3.shape
    k_splits = tuple(a.shape[1] for a in a_list)
    assert sum(k_splits) == K
    tm, tn = _mm_tiles(M, K, N, jnp.dtype(out_dtype).itemsize, res is not None)
    in_specs = [pl.BlockSpec((tm, ks), lambda j, i: (i, 0)) for ks in k_splits]
    in_specs.append(pl.BlockSpec((None, K, tn), lambda j, i: (layer, 0, j)))
    args = list(a_list) + [w3]
    if res is not None:
        in_specs.append(pl.BlockSpec((tm, tn), lambda j, i: (i, j)))
        args.append(res)
    return pl.pallas_call(
        functools.partial(_mm_kernel, k_splits=k_splits, has_res=res is not None),
        grid=(N // tn, M // tm),
        in_specs=in_specs,
        out_specs=pl.BlockSpec((tm, tn), lambda j, i: (i, j)),
        out_shape=jax.ShapeDtypeStruct((M, N), out_dtype),
        scratch_shapes=[pltpu.VMEM((K, tn), BF16)],
        compiler_params=_params(("arbitrary", "arbitrary"), VMEM_LIMIT),
        name="matmul",
    )(*args)


def _conv_kernel(a_ref, g_ref, w_ref, b_ref, ng_ref, nb_ref, o_ref, ybuf, stage, shbuf, *, T, LB):
    i = pl.program_id(1)

    @pl.when(i == 0)
    def _():
        ybuf[0:CONV_HALO, :] = jnp.zeros((CONV_HALO, LB), F32)

    @pl.when(i > 0)
    def _():
        ybuf[0:CONV_HALO, :] = ybuf[T:T + CONV_HALO, :]

    a = a_ref[...].astype(F32)
    g = g_ref[...].astype(F32)
    ybuf[CONV_HALO:CONV_HALO + T, :] = a * jax.nn.sigmoid(g)

    first = CONV_HALO - (CONV_WIDTH - 1)
    n_sub = T // CONV_SUB

    def sub_tile(it, carry):
        r0 = pl.multiple_of(it * CONV_SUB, CONV_SUB)
        stage[...] = ybuf[pl.ds(r0, CONV_SUB + CONV_HALO), :]
        for gi in range(LB // LANES):
            cols = slice(gi * LANES, (gi + 1) * LANES)
            acc = jnp.broadcast_to(b_ref[:, cols], (CONV_SUB, LANES))
            for ph in range(SUBLANES):
                taps = range(ph, CONV_WIDTH, SUBLANES)
                n = CONV_SUB + SUBLANES * (len(taps) - 1)
                sh = shbuf.at[gi, ph % 2]
                sh[0:n, :] = stage[first + ph:first + ph + n, cols]
                for a, j in enumerate(taps):
                    acc = acc + w_ref[j:j + 1, cols] * sh[SUBLANES * a:SUBLANES * a + CONV_SUB, :]
            mu = jnp.mean(acc, axis=-1, keepdims=True)
            d = acc - mu
            var = jnp.mean(d * d, axis=-1, keepdims=True)
            y = d * lax.rsqrt(var + EPS) * ng_ref[:, cols] + nb_ref[:, cols]
            o_ref[pl.ds(r0, CONV_SUB), cols] = (y * jax.nn.sigmoid(y)).astype(o_ref.dtype)
        return carry

    lax.fori_loop(0, n_sub, sub_tile, 0)


def conformer_conv(proj, conv_w, conv_b, norm_g, norm_b, li, T=256, LB=512):
    S = proj.shape[0]
    assert CONV_CH // CONV_GROUPS == LANES
    ncb = CONV_CH // LB
    vec = lambda: pl.BlockSpec((None, 1, LB), lambda c, i: (li, 0, c))
    return pl.pallas_call(
        functools.partial(_conv_kernel, T=T, LB=LB),
        grid=(ncb, S // T),
        in_specs=[pl.BlockSpec((T, LB), lambda c, i: (i, c)),
                  pl.BlockSpec((T, LB), lambda c, i: (i, ncb + c)),
                  pl.BlockSpec((None, CONV_WIDTH, LB), lambda c, i: (li, 0, c)),
                  vec(), vec(), vec()],
        out_specs=pl.BlockSpec((T, LB), lambda c, i: (i, c)),
        out_shape=jax.ShapeDtypeStruct((S, CONV_CH), BF16),
        scratch_shapes=[pltpu.VMEM((CONV_HALO + T, LB), F32),
                        pltpu.VMEM((CONV_HALO + CONV_SUB, LB), F32),
                        pltpu.VMEM((LB // LANES, 2, CONV_HALO + CONV_SUB, LANES), F32)],
        compiler_params=_params(("arbitrary", "arbitrary")),
        name="conformer_conv",
    )(proj, proj, conv_w, conv_b, norm_g, norm_b)


def _diffattn_kernel(q_ref, k_ref, v_ref, lam_ref, g_ref, o_ref, acc_ref, m_ref, l_ref, s0_ref, s1_ref,
                     *, tq, lambda_init):
    qi = pl.program_id(1)
    d = DIFF_HD
    tk = 2 * tq
    c_exp = (d ** -0.5) * LOG2E
    nfull = qi // 2
    acc_ref[...] = jnp.zeros_like(acc_ref)
    m_ref[...] = jnp.full_like(m_ref, -jnp.inf)
    l_ref[...] = jnp.zeros_like(l_ref)

    def qk(t, s_ref):
        kblk = k_ref[pl.ds(pl.multiple_of(t * tk, tk), tk), :]
        for c in range(2):
            s_ref[c] = lax.dot_general(q_ref[:, c * d:(c + 1) * d], kblk[:, c * d:(c + 1) * d],
                                       (((1,), (1,)), ((), ())), preferred_element_type=F32)

    def softmax_pv(t, s_ref, width, masked, row0):
        vblk = v_ref[pl.ds(pl.multiple_of(t * tk, tk), width), :]
        for c in range(2):
            s = s_ref[c, :, 0:width]
            if masked:
                row = lax.broadcasted_iota(jnp.int32, (tq, width), 0) + row0
                col = lax.broadcasted_iota(jnp.int32, (tq, width), 1)
                s = jnp.where(row >= col, s, -jnp.inf)
            m = m_ref[c]
            m_new = jnp.maximum(m, jnp.max(s, axis=-1, keepdims=True))
            alpha = jnp.exp2((m - m_new) * c_exp)
            p = jnp.exp2((s - m_new) * c_exp)
            l_ref[c] = alpha * l_ref[c] + jnp.sum(p, axis=-1, keepdims=True)
            m_ref[c] = m_new
            acc_ref[c] = alpha * acc_ref[c] + jnp.dot(p.astype(BF16), vblk,
                                                      preferred_element_type=F32)

    @pl.when(nfull % 2 == 0)
    def _():
        qk(0, s0_ref)

    @pl.when(nfull % 2 == 1)
    def _():
        qk(0, s1_ref)

    def body(t, carry):
        par = (nfull - t) % 2

        @pl.when(par == 0)
        def _():
            qk(t + 1, s1_ref)
            softmax_pv(t, s0_ref, tk, False, 0)

        @pl.when(par == 1)
        def _():
            qk(t + 1, s0_ref)
            softmax_pv(t, s1_ref, tk, False, 0)
        return carry

    lax.fori_loop(0, nfull, body, 0)

    @pl.when(qi % 2 == 0)
    def _():
        softmax_pv(nfull, s0_ref, tq, True, 0)

    @pl.when(qi % 2 == 1)
    def _():
        softmax_pv(nfull, s0_ref, tk, True, tq)

    lv = lam_ref[...]
    lam = (jnp.exp(jnp.sum(lv[0:1] * lv[1:2], axis=-1, keepdims=True))
           - jnp.exp(jnp.sum(lv[2:3] * lv[3:4], axis=-1, keepdims=True)) + lambda_init)
    o = acc_ref[0] / l_ref[0] - lam * (acc_ref[1] / l_ref[1])
    ms = jnp.mean(o * o, axis=-1, keepdims=True)
    o = o * lax.rsqrt(ms + EPS) * g_ref[...] * (1.0 - lambda_init)
    o_ref[...] = o.astype(o_ref.dtype)


def diff_attention(proj, lam_vecs, subln_g3, li, lambda_init, tq=512):
    S = proj.shape[0]
    assert S % (2 * tq) == 0
    hw = 2 * DIFF_HD
    q0 = 2 * CONV_CH // hw
    k0 = q0 + DIFF_QK // hw
    v0 = k0 + DIFF_QK // hw
    return pl.pallas_call(
        functools.partial(_diffattn_kernel, tq=tq, lambda_init=lambda_init),
        grid=(DIFF_HEADS, S // tq),
        in_specs=[pl.BlockSpec((tq, hw), lambda h, i: (i, q0 + h)),
                  pl.BlockSpec((S, hw), lambda h, i: (0, k0 + h)),
                  pl.BlockSpec((S, hw), lambda h, i: (0, v0 + h)),
                  pl.BlockSpec((None, 4, DIFF_HD), lambda h, i: (li, 0, 0)),
                  pl.BlockSpec((None, 1, hw), lambda h, i: (li, 0, 0))],
        out_specs=pl.BlockSpec((tq, hw), lambda h, i: (i, h)),
        out_shape=jax.ShapeDtypeStruct((S, DIFF_V), BF16),
        scratch_shapes=[pltpu.VMEM((2, tq, hw), F32),
                        pltpu.VMEM((2, tq, 1), F32),
                        pltpu.VMEM((2, tq, 1), F32),
                        pltpu.VMEM((2, tq, 2 * tq), F32),
                        pltpu.VMEM((2, tq, 2 * tq), F32)],
        compiler_params=_params(("arbitrary", "arbitrary"), VMEM_LIMIT),
        name="diff_attention",
    )(proj, proj, proj, lam_vecs, subln_g3)


def _sgu_kernel(u_ref, v_ref, ng_ref, nb_ref, w_ref, bs_ref, o_ref):
    T = SGU_CHUNK
    gw = SGU_CH // SGU_GROUPS
    row = lax.broadcasted_iota(jnp.int32, (T, T), 0)
    col = lax.broadcasted_iota(jnp.int32, (T, T), 1)
    causal = row >= col
    for g in range(SGU_GROUPS):
        cols = slice(g * gw, (g + 1) * gw)
        u = jax.nn.gelu(u_ref[:, cols].astype(F32), approximate=True)
        v = jax.nn.gelu(v_ref[:, cols].astype(F32), approximate=True)
        mu = jnp.mean(v, axis=-1, keepdims=True)
        d = v - mu
        var = jnp.mean(d * d, axis=-1, keepdims=True)
        vn = d * lax.rsqrt(var + EPS) * ng_ref[:, cols] + nb_ref[:, cols]
        w = jnp.where(causal, w_ref[g], 0.0).astype(BF16)
        mixed = jnp.dot(w, vn.astype(BF16), preferred_element_type=F32) + bs_ref[:, g:g + 1]
        o_ref[:, cols] = (u * mixed).astype(o_ref.dtype)


def spatial_gating(proj, norm_g3, norm_b3, w_s, b_s_t, li):
    S = proj.shape[0]
    T = SGU_CHUNK
    vec = lambda: pl.BlockSpec((None, 1, SGU_CH), lambda i: (li, 0, 0))
    return pl.pallas_call(
        _sgu_kernel,
        grid=(S // T,),
        in_specs=[pl.BlockSpec((T, SGU_CH), lambda i: (i, 0)),
                  pl.BlockSpec((T, SGU_CH), lambda i: (i, 1)),
                  vec(), vec(),
                  pl.BlockSpec((None, SGU_GROUPS, T, T), lambda i: (li, 0, 0, 0)),
                  pl.BlockSpec((None, T, SGU_GROUPS), lambda i: (li, 0, 0))],
        out_specs=pl.BlockSpec((T, SGU_CH), lambda i: (i, 0)),
        out_shape=jax.ShapeDtypeStruct((S, SGU_CH), BF16),
        compiler_params=_params(("arbitrary",)),
        name="spatial_gating",
    )(proj, proj, norm_g3, norm_b3, w_s, b_s_t)


def _swa_kernel(sink_ref, q_ref, kp_ref, kc_ref, vp_ref, vc_ref, o_ref, *, li):
    n = pl.program_id(0)
    T = BLOCK
    G = SWA_HEADS // SWA_KV
    d = SWA_HD
    kk = jnp.concatenate([kp_ref[...], kc_ref[...]], axis=0)
    vv = jnp.concatenate([vp_ref[...], vc_ref[...]], axis=0)
    qi = lax.broadcasted_iota(jnp.int32, (T, 2 * T), 0)
    kj = lax.broadcasted_iota(jnp.int32, (T, 2 * T), 1)
    rel = qi + T - kj
    valid = (rel >= 0) & (rel < WINDOW) & ((n * T + kj - T) >= 0)
    scale = d ** -0.5
    for h in range(SWA_KV):
        kh = kk[:, h * d:(h + 1) * d]
        vh = vv[:, h * d:(h + 1) * d]
        outs = []
        for g in range(G):
            c0 = (h * G + g) * d
            s = lax.dot_general(q_ref[:, c0:c0 + d], kh, (((1,), (1,)), ((), ())),
                                preferred_element_type=F32) * scale
            s = jnp.where(valid, s, -jnp.inf)
            sink = sink_ref[li, h * G + g]
            m = jnp.maximum(jnp.max(s, axis=-1, keepdims=True), sink)
            p = jnp.exp(s - m)
            denom = jnp.sum(p, axis=-1, keepdims=True) + jnp.exp(sink - m)
            outs.append(jnp.dot(p.astype(BF16), vh, preferred_element_type=F32) / denom)
        o_ref[:, h * G * d:(h + 1) * G * d] = jnp.concatenate(outs, axis=-1).astype(o_ref.dtype)


def sliding_window_attention(proj, sinks, li):
    S = proj.shape[0]
    T = BLOCK
    qw = SWA_HEADS * SWA_HD
    kw = SWA_KV * SWA_HD
    qb = 2 * SGU_CH // qw
    kb = (2 * SGU_CH + qw) // kw
    vb = kb + 1
    prev = lambda i: jnp.maximum(i - 1, 0)
    return pl.pallas_call(
        functools.partial(_swa_kernel, li=li),
        grid=(S // T,),
        in_specs=[pl.BlockSpec(memory_space=pltpu.SMEM),
                  pl.BlockSpec((T, qw), lambda i: (i, qb)),
                  pl.BlockSpec((T, kw), lambda i: (prev(i), kb)),
                  pl.BlockSpec((T, kw), lambda i: (i, kb)),
                  pl.BlockSpec((T, kw), lambda i: (prev(i), vb)),
                  pl.BlockSpec((T, kw), lambda i: (i, vb))],
        out_specs=pl.BlockSpec((T, qw), lambda i: (i, 0)),
        out_shape=jax.ShapeDtypeStruct((S, qw), BF16),
        compiler_params=_params(("arbitrary",)),
        name="sliding_window_attention",
    )(sinks, proj, proj, proj, proj, proj)


def _xattn_kernel(q_ref, kv_ref, o_ref):
    d = XA_HD
    c_exp = (d ** -0.5) * LOG2E
    for h in range(XA_HEADS):
        q = q_ref[:, h * d:(h + 1) * d]
        k = kv_ref[:, h * d:(h + 1) * d]
        v = kv_ref[:, (XA_HEADS + h) * d:(XA_HEADS + h + 1) * d]
        s = lax.dot_general(q, k, (((1,), (1,)), ((), ())), preferred_element_type=F32)
        m = jnp.max(s, axis=-1, keepdims=True)
        p = jnp.exp2((s - m) * c_exp)
        l = jnp.sum(p, axis=-1, keepdims=True)
        o = jnp.dot(p.astype(BF16), v, preferred_element_type=F32) / l
        o_ref[:, h * d:(h + 1) * d] = o.astype(o_ref.dtype)


def cross_attention(q, kv, tq=512):
    S, qw = q.shape
    M = kv.shape[0]
    return pl.pallas_call(
        _xattn_kernel,
        grid=(S // tq,),
        in_specs=[pl.BlockSpec((tq, qw), lambda i: (i, 0)),
                  pl.BlockSpec((M, 2 * qw), lambda i: (0, 0))],
        out_specs=pl.BlockSpec((tq, qw), lambda i: (i, 0)),
        out_shape=jax.ShapeDtypeStruct((S, qw), BF16),
        compiler_params=_params(("arbitrary",)),
        name="cross_attention",
    )(q, kv)


def _pack_bf16_pair(y):
    c = y.shape[1] // 2
    hi = lax.bitcast_convert_type(y[:, :c].astype(BF16).astype(F32), jnp.uint32)
    lo = lax.bitcast_convert_type(y[:, c:].astype(BF16).astype(F32), jnp.uint32)
    return hi | lax.shift_right_logical(lo, jnp.uint32(16))


def _unpack_bf16_pair(u):
    hi = lax.bitcast_convert_type(u & jnp.uint32(0xFFFF0000), F32)
    lo = lax.bitcast_convert_type(lax.shift_left(u, jnp.uint32(16)), F32)
    return hi, lo


def _router_kernel(h_ref, g_ref, wr_ref, br_ref, hn_ref, route_ref):
    x = h_ref[...]
    ms = jnp.mean(x * x, axis=-1, keepdims=True)
    hn = x * lax.rsqrt(ms + EPS) * g_ref[...]
    packed = _pack_bf16_pair(hn)
    for t in range(hn_ref.shape[1]):
        hn_ref[:, t, :] = packed[:, t * LANES:(t + 1) * LANES]
    hn_hi = hn.astype(BF16)
    hn_lo = (hn - hn_hi.astype(F32)).astype(BF16)
    z = jnp.dot(hn_hi, wr_ref[...], preferred_element_type=F32)
    logits = (z[:, :LANES] + z[:, LANES:]
              + jnp.dot(hn_lo, wr_ref[:, :LANES], preferred_element_type=F32) + br_ref[...])
    tm = x.shape[0]
    lane = lax.broadcasted_iota(jnp.int32, (tm, LANES), 1)
    big = jnp.int32(2 * LANES)
    gl = jnp.where(lane < N_GROUPS, logits, -jnp.inf)
    gmax = jnp.max(gl, axis=-1, keepdims=True)
    gsel = jnp.min(jnp.where(gl == gmax, lane, big), axis=-1, keepdims=True)
    g_w = 1.0 / jnp.sum(jnp.exp(gl - gmax), axis=-1, keepdims=True)
    lo = N_GROUPS + gsel * EXP_PER_GROUP
    el = jnp.where((lane >= lo) & (lane < lo + EXP_PER_GROUP), logits, -jnp.inf)
    e1 = jnp.max(el, axis=-1, keepdims=True)
    i1 = jnp.min(jnp.where(el == e1, lane, big), axis=-1, keepdims=True)
    el2 = jnp.where(lane == i1, -jnp.inf, el)
    e2 = jnp.max(el2, axis=-1, keepdims=True)
    i2 = jnp.min(jnp.where(el2 == e2, lane, big), axis=-1, keepdims=True)
    t = jnp.exp(e2 - e1)
    w1 = g_w / (1.0 + t)
    w2 = g_w * t / (1.0 + t)
    id1 = (i1 - N_GROUPS).astype(F32)
    id2 = (i2 - N_GROUPS).astype(F32)
    route = jnp.where(lane == 0, id1, jnp.where(lane == 1, id2,
                      jnp.where(lane == 2, w1, jnp.where(lane == 3, w2, 0.0))))
    route_ref[...] = route


def moe_router(h, g3, wr, br, layer, tm=256):
    S, D = h.shape
    return pl.pallas_call(
        _router_kernel,
        grid=(S // tm,),
        in_specs=[pl.BlockSpec((tm, D), lambda i: (i, 0)),
                  pl.BlockSpec((None, 1, D), lambda i: (layer, 0, 0)),
                  pl.BlockSpec((D, 2 * LANES), lambda i: (0, 0)),
                  pl.BlockSpec((1, LANES), lambda i: (0, 0))],
        out_specs=[pl.BlockSpec((tm, D // 2 // LANES, LANES), lambda i: (i, 0, 0)),
                   pl.BlockSpec((tm, LANES), lambda i: (i, 0))],
        out_shape=[jax.ShapeDtypeStruct((S, D // 2 // LANES, LANES), jnp.uint32),
                   jax.ShapeDtypeStruct((S, LANES), F32)],
        compiler_params=_params(("arbitrary",), VMEM_LIMIT),
        name="moe_router",
    )(h, g3, wr, br)


def _expert_kernel(be_ref, first_ref, slot_ref, nexte_ref, nu_ref, *refs, layer):
    tok_first = refs[:GATHER_AHEAD]
    (tok_ahead, hn_hbm, wg_hbm, wu_hbm, wd_hbm,
     o_ref, xbuf, wgbuf, wubuf, wdbuf, xsem, wsem) = refs[GATHER_AHEAD:]
    nslot = GATHER_AHEAD + 1
    b = pl.program_id(0)
    nu = nu_ref[0]
    xs = b % nslot

    def row_copy(t, s, r):
        return pltpu.make_async_copy(hn_hbm.at[pl.ds(t, 1)], xbuf.at[s, pl.ds(r, 1)], xsem.at[s])

    def start_rows(tok_ref, s):
        def body(j, c):
            row_copy(tok_ref[0, 2 * j], s, 2 * j).start(priority=0)
            row_copy(tok_ref[0, 2 * j + 1], s, 2 * j + 1).start(priority=1)
            return c
        lax.fori_loop(0, MOE_BLOCK // 2, body, 0, unroll=4)

    @pl.when(b == 0)
    def _():
        for k in range(GATHER_AHEAD):
            pl.when(k < nu)(functools.partial(start_rows, tok_first[k], k))

    @pl.when(b + GATHER_AHEAD < nu)
    def _():
        start_rows(tok_ahead, (b + GATHER_AHEAD) % nslot)

    def w_copies(e, s):
        out = []
        for m, (src, dst) in enumerate(((wg_hbm, wgbuf), (wu_hbm, wubuf), (wd_hbm, wdbuf))):
            rows = dst.shape[1] // W_CHUNKS
            for k in range(W_CHUNKS):
                rs = pl.ds(k * rows, rows)
                out.append(pltpu.make_async_copy(src.at[layer, e, rs], dst.at[s, rs], wsem.at[s, m]))
        return out

    def start_weights(e, s):
        for k, c in enumerate(w_copies(e, s)):
            c.start(priority=k % 2)

    @pl.when(b == 0)
    def _():
        start_weights(be_ref[0], 0)

    @pl.when(b < nu)
    def _():
        s = slot_ref[b]

        @pl.when(first_ref[b] == 1)
        def _():
            for c in w_copies(be_ref[b], s):
                c.wait()
            ne = nexte_ref[b]

            @pl.when(ne >= 0)
            def _():
                start_weights(ne, 1 - s)

        def wait_body(r, c):
            row_copy(0, xs, r).wait()
            return c
        lax.fori_loop(0, MOE_BLOCK, wait_body, 0, unroll=8)
        x_u32 = jnp.concatenate([xbuf[xs, :, t, :] for t in range(xbuf.shape[2])], axis=-1)
        x_hi, x_lo = _unpack_bf16_pair(x_u32)
        half = x_hi.shape[1]
        hg = (jnp.dot(x_hi, wgbuf[s, 0:half, :], preferred_element_type=F32)
              + jnp.dot(x_lo, wgbuf[s, half:, :], preferred_element_type=F32))
        hu = (jnp.dot(x_hi, wubuf[s, 0:half, :], preferred_element_type=F32)
              + jnp.dot(x_lo, wubuf[s, half:, :], preferred_element_type=F32))
        hb = (hg * jax.nn.sigmoid(hg) * hu).astype(BF16).astype(F32)
        o_ref[...] = _pack_bf16_pair(jnp.dot(hb, wdbuf[s], preferred_element_type=F32))

    @pl.when(b >= nu)
    def _():
        o_ref[...] = jnp.zeros_like(o_ref)


def moe_experts(hn_slabs, plan, w_gate, w_up, w_down, layer):
    S, RT, _ = hn_slabs.shape
    C = RT * LANES
    D = 2 * C
    nblk = plan["row_tok"].shape[0]
    assert nblk > GATHER_AHEAD
    tok_spec = lambda f: pl.BlockSpec((None, 1, MOE_BLOCK), f, memory_space=pltpu.SMEM)
    fixed = [tok_spec(functools.partial(lambda k, b, *_: (k, 0, 0), k)) for k in range(GATHER_AHEAD)]
    ahead = tok_spec(lambda b, *_: (jnp.minimum(b + GATHER_AHEAD, nblk - 1), 0, 0))
    anyspec = pl.BlockSpec(memory_space=pl.ANY)
    grid_spec = pltpu.PrefetchScalarGridSpec(
        num_scalar_prefetch=5,
        grid=(nblk,),
        in_specs=fixed + [ahead, anyspec, anyspec, anyspec, anyspec],
        out_specs=pl.BlockSpec((MOE_BLOCK, C), lambda b, *_: (b, 0)),
        scratch_shapes=[pltpu.VMEM((GATHER_AHEAD + 1, MOE_BLOCK, RT, LANES), jnp.uint32),
                        pltpu.VMEM((2, D, D_FF), F32),
                        pltpu.VMEM((2, D, D_FF), F32),
                        pltpu.VMEM((2, D_FF, D), F32),
                        pltpu.SemaphoreType.DMA((GATHER_AHEAD + 1,)),
                        pltpu.SemaphoreType.DMA((2, 3))],
    )
    return pl.pallas_call(
        functools.partial(_expert_kernel, layer=layer),
        grid_spec=grid_spec,
        out_shape=jax.ShapeDtypeStruct((nblk * MOE_BLOCK, C), jnp.uint32),
        compiler_params=_params(("arbitrary",), VMEM_LIMIT),
        name="moe_experts",
    )(plan["blk_e"], plan["first"], plan["slot"], plan["next_e"], plan["n_used"],
      *([plan["row_tok"]] * (GATHER_AHEAD + 1)), hn_slabs, w_gate, w_up, w_down)


def _combine_kernel(*refs, tm, nt, emit_h):
    pos_first = refs[:GATHER_AHEAD]
    pos_ahead, h_ref, route_ref, g_ref, y_hbm = refs[GATHER_AHEAD:GATHER_AHEAD + 5]
    rest = refs[GATHER_AHEAD + 5:]
    if emit_h:
        h_out_ref, n_out_ref, ybuf, sem = rest
    else:
        n_out_ref, ybuf, sem = rest
    nslot = GATHER_AHEAD + 1
    i = pl.program_id(0)
    xs = i % nslot

    def row_copy(p, s, j):
        return pltpu.make_async_copy(y_hbm.at[pl.ds(p, 1), :], ybuf.at[s, pl.ds(j, 1), :], sem.at[s])

    def start_rows(pos_ref, s):
        def body(j, c):
            row_copy(pos_ref[0, 2 * j], s, 2 * j).start(priority=0)
            row_copy(pos_ref[0, 2 * j + 1], s, 2 * j + 1).start(priority=1)
            return c
        lax.fori_loop(0, TOP_K * tm // 2, body, 0, unroll=4)

    @pl.when(i == 0)
    def _():
        for k in range(GATHER_AHEAD):
            start_rows(pos_first[k], k)

    @pl.when(i + GATHER_AHEAD < nt)
    def _():
        start_rows(pos_ahead, (i + GATHER_AHEAD) % nslot)

    def wait_body(j, c):
        row_copy(0, xs, j).wait()
        return c
    lax.fori_loop(0, TOP_K * tm, wait_body, 0, unroll=8)

    a_hi, a_lo = _unpack_bf16_pair(ybuf[xs, 0:tm, :])
    b_hi, b_lo = _unpack_bf16_pair(ybuf[xs, tm:2 * tm, :])
    w1 = route_ref[:, TOP_K:TOP_K + 1]
    w2 = route_ref[:, TOP_K + 1:TOP_K + 2]
    half = a_hi.shape[1]
    h_lo = h_ref[:, 0:half] + w1 * a_hi + w2 * b_hi
    h_hi = h_ref[:, half:] + w1 * a_lo + w2 * b_lo
    if emit_h:
        h_out_ref[:, 0:half] = h_lo
        h_out_ref[:, half:] = h_hi
    ms = (jnp.sum(h_lo * h_lo, axis=-1, keepdims=True)
          + jnp.sum(h_hi * h_hi, axis=-1, keepdims=True)) / (2 * half)
    r = lax.rsqrt(ms + EPS)
    n_out_ref[:, 0:half] = (h_lo * r * g_ref[:, 0:half]).astype(n_out_ref.dtype)
    n_out_ref[:, half:] = (h_hi * r * g_ref[:, half:]).astype(n_out_ref.dtype)


def moe_combine(h, route, yb, pos3, g3, g_layer, norm_dtype, emit_h, tm=128):
    S, D = h.shape
    nt = S // tm
    assert nt > GATHER_AHEAD
    pos_spec = lambda f: pl.BlockSpec((None, 1, TOP_K * tm), f, memory_space=pltpu.SMEM)
    out_specs = [pl.BlockSpec((tm, D), lambda i: (i, 0))]
    out_shape = [jax.ShapeDtypeStruct((S, D), norm_dtype)]
    if emit_h:
        out_specs = [pl.BlockSpec((tm, D), lambda i: (i, 0))] + out_specs
        out_shape = [jax.ShapeDtypeStruct((S, D), F32)] + out_shape
    return pl.pallas_call(
        functools.partial(_combine_kernel, tm=tm, nt=nt, emit_h=emit_h),
        grid=(nt,),
        in_specs=[pos_spec(functools.partial(lambda k, i: (k, 0, 0), k)) for k in range(GATHER_AHEAD)] + [
                  pos_spec(lambda i: (jnp.minimum(i + GATHER_AHEAD, nt - 1), 0, 0)),
                  pl.BlockSpec((tm, D), lambda i: (i, 0)),
                  pl.BlockSpec((tm, LANES), lambda i: (i, 0)),
                  pl.BlockSpec((None, 1, D), lambda i: (g_layer, 0, 0)),
                  pl.BlockSpec(memory_space=pl.ANY)],
        out_specs=out_specs,
        out_shape=out_shape,
        scratch_shapes=[pltpu.VMEM((GATHER_AHEAD + 1, TOP_K * tm, D // 2), jnp.uint32),
                        pltpu.SemaphoreType.DMA((GATHER_AHEAD + 1,))],
        compiler_params=_params(("arbitrary",), VMEM_LIMIT),
        name="moe_combine",
    )(*([pos3] * (GATHER_AHEAD + 1)), h, route, g3, yb)


def hierarchical_moe(h, g3, wr_g, br_g, wr_e, br_e, w_gate, w_up, w_down, layer,
                     next_g3, next_layer, next_dtype, emit_h, tm_combine=256):
    S, D = h.shape
    wr = jnp.concatenate([wr_g[layer], jnp.transpose(wr_e[layer], (1, 0, 2)).reshape(D, N_EXPERTS)],
                         axis=1)
    wr = jnp.pad(wr, ((0, 0), (0, LANES - wr.shape[1])))
    wr_hi = wr.astype(BF16)
    wr = jnp.concatenate([wr_hi, (wr - wr_hi.astype(F32)).astype(BF16)], axis=1)
    br =jnp.concatenate([br_g[layer], br_e[layer].reshape(-1)])
    br = jnp.pad(br, (0, LANES - br.shape[0])).reshape(1, LANES)
    hn_slabs, route = moe_router(h, g3, wr, br, layer)
    plan = _moe_plan(route[:, 0:TOP_K].astype(jnp.int32), tm_combine)
    yb = moe_experts(hn_slabs, plan, w_gate, w_up, w_down, layer)
    return moe_combine(h, route, yb, plan["pos3"], next_g3, next_layer, next_dtype, emit_h,
                       tm=tm_combine)


def _moe_plan(expert_id, tm_combine):
    S = expert_id.shape[0]
    A = S * TOP_K
    i32 = jnp.int32
    flat_e = expert_id.reshape(-1)
    order = jnp.argsort(flat_e).astype(i32)
    rank = jnp.argsort(order).astype(i32)
    edges = jnp.searchsorted(flat_e[order], jnp.arange(N_EXPERTS + 1, dtype=i32), side='left').astype(i32)
    start, counts = edges[:-1], edges[1:] - edges[:-1]
    nb_e = (counts + MOE_BLOCK - 1) // MOE_BLOCK
    bend = jnp.cumsum(nb_e)
    pstart = (bend - nb_e) * MOE_BLOCK
    shift = pstart - start
    dshift = shift - jnp.concatenate([jnp.zeros((1,), i32), shift[:-1]])
    pos_flat = rank + jnp.sum(jnp.where(rank[:, None] >= start[None, :], dshift[None, :], 0), axis=1)
    nblk = -(-A // MOE_BLOCK) + N_EXPERTS
    n_used = bend[-1]
    ar = jnp.arange(nblk, dtype=i32)
    blk_e = jnp.minimum(jnp.searchsorted(bend, ar, side='right'), N_EXPERTS - 1).astype(i32)
    base = start[blk_e] + (ar - (bend - nb_e)[blk_e]) * MOE_BLOCK
    sidx = base[:, None] + jnp.arange(MOE_BLOCK, dtype=i32)[None, :]
    live = (sidx < (start + counts)[blk_e][:, None]) & (ar < n_used)[:, None]
    row_tok = jnp.where(live, order[jnp.clip(sidx, 0, A - 1)] // TOP_K, 0)
    first = ((ar == 0) | (blk_e != jnp.roll(blk_e, 1))) & (ar < n_used)
    slot = (jnp.cumsum(first.astype(i32)) - 1) % 2
    big = i32(nblk)
    nf = lax.cummin(jnp.where(first, ar, big)[::-1])[::-1]
    next_first = jnp.concatenate([nf[1:], big[None]])
    next_e = jnp.where(next_first < big, blk_e[jnp.minimum(next_first, nblk - 1)], -1)
    nt = S // tm_combine
    pos3 = pos_flat.reshape(nt, tm_combine, TOP_K).transpose(0, 2, 1).reshape(nt, 1, TOP_K * tm_combine)
    return dict(row_tok=row_tok.reshape(nblk, 1, MOE_BLOCK), pos3=pos3, blk_e=blk_e,
                first=first.astype(i32), slot=slot.astype(i32), next_e=next_e.astype(i32),
                n_used=n_used.astype(i32).reshape(1))


def kernel(x, mem, mix_norm, even_w_in, conv_w, conv_b, conv_norm_g, conv_norm_b, diff_lambda, diff_subln_g, even_w_out, odd_w_in, sgu_norm_g, sgu_norm_b, sgu_w, sgu_b, swa_sinks, odd_w_out, xa_norm, mem_norm, xa_wq, xa_wkv, xa_wo, moe_norm, router_group_w, router_group_b, router_expert_w, router_expert_b, w_gate, w_up, w_down, final_norm):
    B, S, D = x.shape
    assert B == 1
    row3 = lambda p: p.reshape(p.shape[0], 1, p.shape[1])
    h = x.reshape(S, D)
    mem2 = mem.reshape(mem.shape[1], D)
    hn = rmsnorm(h, row3(mix_norm), 0, BF16)
    for layer in range(DEPTH):
        i = layer // 2
        if layer % 2 == 0:
            proj = matmul([hn], even_w_in, i, out_dtype=BF16)
            y_a = conformer_conv(proj, conv_w, row3(conv_b), row3(conv_norm_g), row3(conv_norm_b), i)
            lambda_init = 0.8 - 0.6 * math.exp(-0.3 * layer)
            y_b = diff_attention(proj, diff_lambda, row3(diff_subln_g), i, lambda_init)
            h = matmul([y_a, y_b], even_w_out, i, res=h, out_dtype=F32)
        else:
            proj = matmul([hn], odd_w_in, i, out_dtype=BF16)
            y_c = spatial_gating(proj, row3(sgu_norm_g), row3(sgu_norm_b), sgu_w,
                                 jnp.transpose(sgu_b, (0, 2, 1)), i)
            y_d = sliding_window_attention(proj, swa_sinks, i)
            h = matmul([y_c, y_d], odd_w_out, i, res=h, out_dtype=F32)
        hq = rmsnorm(h, row3(xa_norm), layer, BF16)
        mem_n = rmsnorm(mem2, row3(mem_norm), layer, BF16)
        q = matmul([hq], xa_wq, layer, out_dtype=BF16)
        kv = matmul([mem_n], xa_wkv, layer, out_dtype=BF16)
        o = cross_attention(q, kv)
        h = matmul([o], xa_wo, layer, res=h, out_dtype=F32)
        last = layer == DEPTH - 1
        moe_args = (h, row3(moe_norm), router_group_w, router_group_b, router_expert_w,
                    router_expert_b, w_gate, w_up, w_down, layer)
        if last:
            (out,) = hierarchical_moe(*moe_args, final_norm.reshape(1, 1, D), 0, F32, False)
        else:
            h, hn = hierarchical_moe(*moe_args, row3(mix_norm), layer + 1, BF16, True)
    return out.reshape(B, S, D)
```
